```python
import math
import jax, jax.numpy as jnp
from jax import lax
import numpy as np

D_MODEL = 2048
BATCH = 2
SEQ = 8192
DEPTH = 2

N_A_LAYERS = DEPTH // 2
N_B_LAYERS = DEPTH - N_A_LAYERS
SSM_GROUP = 16
SSM_GROUPS = D_MODEL // SSM_GROUP
SSM_STATE = 64
DT_MIN = 1e-3
DT_MAX = 1e-1
HEAD_DIM = 64
N_Q_HEADS = D_MODEL // HEAD_DIM
N_KV_HEADS = N_Q_HEADS // 8
Q_PER_KV = N_Q_HEADS // N_KV_HEADS
WINDOW = 128
BLOCK = 128
ROPE_DIM = HEAD_DIM // 4
ROPE_THETA = 500000.0
D_FF = 5632
FFN_RESIDUAL_WEIGHT = 0.5
NORM_EPS = 1e-6
MASK_VALUE = -1e30

kernel_name = "yoco_s5_swa_sink_macaron_sandwich"


def rms_norm(x, g):
    xf = x.astype(jnp.float32)
    y = xf * lax.rsqrt(jnp.mean(xf * xf, axis=-1, keepdims=True) + NORM_EPS)
    return (y * g.astype(jnp.float32)).astype(x.dtype)


def swiglu(x, w_gate, w_up, w_down):
    return (jax.nn.silu(x @ w_gate) * (x @ w_up)) @ w_down


def rotary_partial(t, positions):
    half = ROPE_DIM // 2
    inv_freq = ROPE_THETA ** (-jnp.arange(half, dtype=jnp.float32) / half)
    ang = positions.astype(jnp.float32)[:, None] * inv_freq[None, :]
    cos = jnp.cos(ang)[None, :, None, :]
    sin = jnp.sin(ang)[None, :, None, :]
    tr = t[..., :ROPE_DIM].astype(jnp.float32)
    t1, t2 = tr[..., :half], tr[..., half:]
    rot = jnp.concatenate([t1 * cos - t2 * sin, t2 * cos + t1 * sin], axis=-1).astype(t.dtype)
    return jnp.concatenate([rot, t[..., ROPE_DIM:]], axis=-1)


def s5_mixer(u, a_re, a_im, log_dt, b_re, b_im, c_re, c_im, d_skip, glu_w_out, glu_w_gate, glu_b_gate):
    f32 = jnp.float32
    bsz, s, _ = u.shape
    uf = u.astype(f32).reshape(bsz, s, SSM_GROUPS, SSM_GROUP)
    a_re = a_re.astype(f32)
    a_im = a_im.astype(f32)
    dt = jnp.exp(log_dt.astype(f32))[:, None]
    mag = jnp.exp(a_re * dt)
    lb_re = mag * jnp.cos(a_im * dt)
    lb_im = mag * jnp.sin(a_im * dt)
    den = a_re * a_re + a_im * a_im
    num_re = lb_re - 1.0
    f_re = (num_re * a_re + lb_im * a_im) / den
    f_im = (lb_im * a_re - num_re * a_im) / den
    b_re = b_re.astype(f32)
    b_im = b_im.astype(f32)
    bb_re = f_re[..., None] * b_re - f_im[..., None] * b_im
    bb_im = f_re[..., None] * b_im + f_im[..., None] * b_re
    bu_re = jnp.einsum('bsgc,gpc->bsgp', uf, bb_re)
    bu_im = jnp.einsum('bsgc,gpc->bsgp', uf, bb_im)
    a_el_re = jnp.broadcast_to(lb_re[None, None], (1, s, SSM_GROUPS, SSM_STATE))
    a_el_im = jnp.broadcast_to(lb_im[None, None], (1, s, SSM_GROUPS, SSM_STATE))

    def combine(e1, e2):
        a1r, a1i, b1r, b1i = e1
        a2r, a2i, b2r, b2i = e2
        return (a2r * a1r - a2i * a1i,
                a2r * a1i + a2i * a1r,
                a2r * b1r - a2i * b1i + b2r,
                a2r * b1i + a2i * b1r + b2i)

    _, _, x_re, x_im = lax.associative_scan(combine, (a_el_re, a_el_im, bu_re, bu_im), axis=1)
    y = (jnp.einsum('bsgp,gcp->bsgc', x_re, c_re.astype(f32))
         - jnp.einsum('bsgp,gcp->bsgc', x_im, c_im.astype(f32))
         + d_skip.astype(f32).reshape(SSM_GROUPS, SSM_GROUP) * uf)
    z = jax.nn.gelu(y.reshape(bsz, s, D_MODEL)).astype(u.dtype)
    return (z @ glu_w_out) * jax.nn.sigmoid(z @ glu_w_gate + glu_b_gate)


def shared_kv(h, kv_norm_g, w_kv, b_kv, positions):
    bsz, s, _ = h.shape
    n_blocks = s // BLOCK
    kv = rms_norm(h, kv_norm_g) @ w_kv + b_kv
    k, v = jnp.split(kv, 2, axis=-1)
    k = rotary_partial(k.reshape(bsz, s, N_KV_HEADS, HEAD_DIM), positions)
    v = v.reshape(bsz, s, N_KV_HEADS, HEAD_DIM)

    def band(t):
        tb = t.reshape(bsz, n_blocks, BLOCK, N_KV_HEADS, HEAD_DIM)
        prev = jnp.pad(tb[:, :-1], ((0, 0), (1, 0), (0, 0), (0, 0), (0, 0)))
        return jnp.concatenate([prev, tb], axis=2)

    return band(k), band(v)


def swa_sink_mixer(hn, k_band, v_band, w_q, b_q, sinks, w_o, b_o, positions):
    bsz, s, _ = hn.shape
    n_blocks = s // BLOCK
    q = rotary_partial((hn @ w_q + b_q).reshape(bsz, s, N_Q_HEADS, HEAD_DIM), positions)
    q = q.reshape(bsz, n_blocks, BLOCK, N_KV_HEADS, Q_PER_KV, HEAD_DIM)
    scores = jnp.einsum('bnqkgd,bnjkd->bnkgqj', q, k_band).astype(jnp.float32) * (HEAD_DIM ** -0.5)
    qi = jnp.arange(BLOCK)[:, None]
    kj = jnp.arange(2 * BLOCK)[None, :]
    diff = qi + BLOCK - kj
    band_ok = (diff >= 0) & (diff < WINDOW)
    blk = jnp.arange(n_blocks)[:, None, None]
    valid = band_ok[None] & ((blk > 0) | (kj >= BLOCK)[None])
    scores = jnp.where(valid[None, :, None, None], scores, MASK_VALUE)
    sink = jnp.broadcast_to(sinks.astype(jnp.float32).reshape(N_KV_HEADS, Q_PER_KV)[None, None, :, :, None, None],
                            scores.shape[:-1] + (1,))
    probs = jax.nn.softmax(jnp.concatenate([scores, sink], axis=-1), axis=-1)[..., :-1]
    o = jnp.einsum('bnkgqj,bnjkd->bnqkgd', probs.astype(v_band.dtype), v_band)
    return o.reshape(bsz, s, N_Q_HEADS * HEAD_DIM) @ w_o + b_o


def setup_inputs(seed: int = 0) -> dict:
    key = jax.random.key(seed)
    ks = jax.random.split(key, 26)
    f32 = jnp.float32

    def nrm(k, shape, fan_in):
        return jax.random.normal(k, shape, f32) * (fan_in ** -0.5)

    def small(k, shape, scale=0.02):
        return jax.random.normal(k, shape, f32) * scale

    ga, p = SSM_GROUPS, SSM_STATE
    n_idx = jnp.arange(p, dtype=f32)
    return {
        "x": jax.random.normal(ks[0], (BATCH, SEQ, D_MODEL), f32),
        "norm_g": 1.0 + small(ks[1], (DEPTH, 6, D_MODEL)),
        "ffn_w_gate": nrm(ks[2], (DEPTH, 2, D_MODEL, D_FF), D_MODEL),
        "ffn_w_up": nrm(ks[3], (DEPTH, 2, D_MODEL, D_FF), D_MODEL),
        "ffn_w_down": nrm(ks[4], (DEPTH, 2, D_FF, D_MODEL), D_FF),
        "ssm_a_re": -0.5 + small(ks[5], (N_A_LAYERS, ga, p), 0.01),
        "ssm_a_im": math.pi * n_idx + small(ks[6], (N_A_LAYERS, ga, p), 0.01),
        "ssm_log_dt": jax.random.uniform(ks[7], (N_A_LAYERS, ga), f32, math.log(DT_MIN), math.log(DT_MAX)),
        "ssm_b_re": nrm(ks[8], (N_A_LAYERS, ga, p, SSM_GROUP), 2 * SSM_GROUP),
        "ssm_b_im": nrm(ks[9], (N_A_LAYERS, ga, p, SSM_GROUP), 2 * SSM_GROUP),
        "ssm_c_re": nrm(ks[10], (N_A_LAYERS, ga, SSM_GROUP, p), p),
        "ssm_c_im": nrm(ks[11], (N_A_LAYERS, ga, SSM_GROUP, p), p),
        "ssm_d": jax.random.normal(ks[12], (N_A_LAYERS, D_MODEL), f32),
        "glu_w_out": nrm(ks[13], (N_A_LAYERS, D_MODEL, D_MODEL), D_MODEL),
        "glu_w_gate": nrm(ks[14], (N_A_LAYERS, D_MODEL, D_MODEL), D_MODEL),
        "glu_b_gate": small(ks[15], (N_A_LAYERS, D_MODEL)),
        "kv_norm_g": 1.0 + small(ks[16], (D_MODEL,)),
        "w_kv": nrm(ks[17], (D_MODEL, 2 * N_KV_HEADS * HEAD_DIM), D_MODEL),
        "b_kv": small(ks[18], (2 * N_KV_HEADS * HEAD_DIM,)),
        "w_q": nrm(ks[19], (N_B_LAYERS, D_MODEL, N_Q_HEADS * HEAD_DIM), D_MODEL),
        "b_q": small(ks[20], (N_B_LAYERS, N_Q_HEADS * HEAD_DIM)),
        "attn_sinks": jax.random.normal(ks[21], (N_B_LAYERS, N_Q_HEADS), f32) * 0.5,
        "w_o": nrm(ks[22], (N_B_LAYERS, N_Q_HEADS * HEAD_DIM, D_MODEL), N_Q_HEADS * HEAD_DIM),
        "b_o": small(ks[23], (N_B_LAYERS, D_MODEL)),
    }


def reference(x, norm_g, ffn_w_gate, ffn_w_up, ffn_w_down, ssm_a_re, ssm_a_im, ssm_log_dt, ssm_b_re, ssm_b_im,
              ssm_c_re, ssm_c_im, ssm_d, glu_w_out, glu_w_gate, glu_b_gate, kv_norm_g, w_kv, b_kv, w_q, b_q,
              attn_sinks, w_o, b_o):
    positions = jnp.arange(x.shape[1], dtype=jnp.int32)
    h = x
    k_band = None
    v_band = None
    for layer in range(DEPTH):
        g = norm_g[layer]
        h = h + FFN_RESIDUAL_WEIGHT * rms_norm(
            swiglu(rms_norm(h, g[0]), ffn_w_gate[layer, 0], ffn_w_up[layer, 0], ffn_w_down[layer, 0]), g[1])
        hn = rms_norm(h, g[2])
        if layer < N_A_LAYERS:
            mix = s5_mixer(hn, ssm_a_re[layer], ssm_a_im[layer], ssm_log_dt[layer], ssm_b_re[layer],
                           ssm_b_im[layer], ssm_c_re[layer], ssm_c_im[layer], ssm_d[layer],
                           glu_w_out[layer], glu_w_gate[layer], glu_b_gate[layer])
        else:
            bl = layer - N_A_LAYERS
            mix = swa_sink_mixer(hn, k_band, v_band, w_q[bl], b_q[bl], attn_sinks[bl], w_o[bl], b_o[bl], positions)
        h = h + rms_norm(mix, g[3])
        h = h + FFN_RESIDUAL_WEIGHT * rms_norm(
            swiglu(rms_norm(h, g[4]), ffn_w_gate[layer, 1], ffn_w_up[layer, 1], ffn_w_down[layer, 1]), g[5])
        if layer == N_A_LAYERS - 1:
            k_band, v_band = shared_kv(h, kv_norm_g, w_kv, b_kv, positions)
    return h
```

```python
import functools
import math

import jax
import jax.numpy as jnp
from jax import lax
from jax.experimental import pallas as pl
from jax.experimental.pallas import tpu as pltpu

F32 = jnp.float32
BF16 = jnp.bfloat16

NORM_EPS = 1e-6
FFN_RESIDUAL_WEIGHT = 0.5
SSM_GROUP = 16
SSM_CHUNK = 16
HEAD_DIM = 64
Q_PER_KV = 8
ATTN_BLOCK = 128
ROPE_DIM = HEAD_DIM // 4
ROPE_THETA = 500000.0
MASK_VALUE = -1e30
LANES = 128
VMEM_LIMIT_BYTES = 56 * 1024 * 1024


def _compiler_params(semantics):
    return pltpu.CompilerParams(dimension_semantics=semantics, vmem_limit_bytes=VMEM_LIMIT_BYTES)


def _rms_norm(x, g):
    return x * lax.rsqrt(jnp.mean(x * x, axis=-1, keepdims=True) + NORM_EPS) * g


def _resident(shape):
    return pl.BlockSpec(shape, lambda *_: (0,) * len(shape), pipeline_mode=pl.Buffered(1))


def _row_tile(t, want):
    tm = min(t, want)
    assert t % tm == 0
    return tm


def _ffn_kernel(emit_next, h_ref, gpre_ref, gpost_ref, *rest):
    if emit_next:
        gnext_ref, wg_ref, wu_ref, wd_ref, o_ref, on_ref, xn_ref, acc_ref = rest
    else:
        wg_ref, wu_ref, wd_ref, o_ref, xn_ref, acc_ref = rest
    j = pl.program_id(1)

    @pl.when(j == 0)
    def _():
        xn_ref[...] = _rms_norm(h_ref[...], gpre_ref[...]).astype(BF16)

    xn = xn_ref[...]
    gate = jnp.dot(xn, wg_ref[...], preferred_element_type=F32)
    up = jnp.dot(xn, wu_ref[...], preferred_element_type=F32)
    act = (gate * jax.nn.sigmoid(gate) * up).astype(BF16)
    part = jnp.dot(act, wd_ref[...], preferred_element_type=F32)

    @pl.when(j == 0)
    def _():
        acc_ref[...] = part

    @pl.when(j > 0)
    def _():
        acc_ref[...] += part

    @pl.when(j == pl.num_programs(1) - 1)
    def _():
        h_new = h_ref[...] + FFN_RESIDUAL_WEIGHT * _rms_norm(acc_ref[...], gpost_ref[...])
        o_ref[...] = h_new
        if emit_next:
            on_ref[...] = _rms_norm(h_new, gnext_ref[...]).astype(BF16)


def _ffn(h, g_pre, g_post, w_gate, w_up, w_down, g_next=None, tm=512, tf=512):
    t, d = h.shape
    f = w_gate.shape[1]
    tm = _row_tile(t, tm)
    tf = _row_tile(f, tf)
    emit_next = g_next is not None
    row = pl.BlockSpec((tm, d), lambda i, j: (i, 0))
    vec = pl.BlockSpec((1, d), lambda i, j: (0, 0))
    in_specs = [row, vec, vec] + ([vec] if emit_next else []) + [
        pl.BlockSpec((d, tf), lambda i, j: (0, j)),
        pl.BlockSpec((d, tf), lambda i, j: (0, j)),
        pl.BlockSpec((tf, d), lambda i, j: (j, 0)),
    ]
    args = [h, g_pre.reshape(1, d), g_post.reshape(1, d)] + ([g_next.reshape(1, d)] if emit_next else []) + [
        w_gate, w_up, w_down]
    out_shape = [jax.ShapeDtypeStruct((t, d), F32)] + ([jax.ShapeDtypeStruct((t, d), BF16)] if emit_next else [])
    out_specs = [row] + ([row] if emit_next else [])
    outs = pl.pallas_call(
        functools.partial(_ffn_kernel, emit_next),
        grid=(t // tm, f // tf),
        in_specs=in_specs,
        out_specs=out_specs,
        out_shape=out_shape,
        scratch_shapes=[pltpu.VMEM((tm, d), BF16), pltpu.VMEM((tm, d), F32)],
        compiler_params=_compiler_params(("parallel", "arbitrary")),
        name="ffn_next" if emit_next else "ffn",
    )(*args)
    return outs if emit_next else outs[0]


def _s5_param_kernel(zr_ref, zi_ref, bbr_x_ref, bbi_x_ref, cr_x_ref, ci_x_ref,
                     zrev_r_ref, zrev_i_ref, bbt_r_ref, bbt_i_ref, z1_r_ref, z1_i_ref,
                     k_ref, bm_r_ref, bm_i_ref, cm_r_ref, cm_i_ref):
    bbr, bbi, cr, ci = bbr_x_ref[...], bbi_x_ref[...], cr_x_ref[...], ci_x_ref[...]
    w1 = bbr * cr - bbi * ci
    w2 = bbi * cr + bbr * ci
    dn = (((2,), (1,)), ((0,), (0,)))
    k_ref[...] = (lax.dot_general(zr_ref[...], w1, dn, precision=lax.Precision.HIGHEST, preferred_element_type=F32)
                  - lax.dot_general(zi_ref[...], w2, dn, precision=lax.Precision.HIGHEST,
                                    preferred_element_type=F32))
    zvr, zvi, btr, bti = zrev_r_ref[...], zrev_i_ref[...], bbt_r_ref[...], bbt_i_ref[...]
    bm_r_ref[...] = zvr * btr - zvi * bti
    bm_i_ref[...] = zvr * bti + zvi * btr
    z1r, z1i = z1_r_ref[...], z1_i_ref[...]
    cm_r_ref[...] = cr * z1r - ci * z1i
    cm_i_ref[...] = -(cr * z1i + ci * z1r)


def _s5_params(a_re, a_im, log_dt, b_re, b_im, c_re, c_im):
    g, p = a_re.shape
    gc, ch = SSM_GROUP, SSM_CHUNK
    n = ch * gc
    dt = jnp.exp(log_dt.astype(F32))[:, None]
    lam_re = a_re.astype(F32) * dt
    lam_im = a_im.astype(F32) * dt
    lags = jnp.arange(ch + 1, dtype=F32)[None, :, None]
    mag = jnp.exp(lags * lam_re[:, None, :])
    zr = mag * jnp.cos(lags * lam_im[:, None, :])
    zi = mag * jnp.sin(lags * lam_im[:, None, :])
    lb_re, lb_im = zr[:, 1], zi[:, 1]
    den = a_re * a_re + a_im * a_im
    num_re = lb_re - 1.0
    f_re = (num_re * a_re + lb_im * a_im) / den
    f_im = (lb_im * a_re - num_re * a_im) / den
    bb_re = f_re[..., None] * b_re - f_im[..., None] * b_im
    bb_im = f_re[..., None] * b_im + f_im[..., None] * b_re

    bbr_x = jnp.repeat(bb_re, gc, axis=-1)
    bbi_x = jnp.repeat(bb_im, gc, axis=-1)
    cr_x = jnp.tile(jnp.swapaxes(c_re, 1, 2), (1, 1, ch))
    ci_x = jnp.tile(jnp.swapaxes(c_im, 1, 2), (1, 1, ch))
    zrev_r = jnp.repeat(zr[:, ch - 1::-1][:, :ch], gc, axis=1)
    zrev_i = jnp.repeat(zi[:, ch - 1::-1][:, :ch], gc, axis=1)
    bbt_r = jnp.tile(jnp.swapaxes(bb_re, 1, 2), (1, ch, 1))
    bbt_i = jnp.tile(jnp.swapaxes(bb_im, 1, 2), (1, ch, 1))
    z1_r = jnp.repeat(jnp.swapaxes(zr[:, 1:], 1, 2), gc, axis=-1)
    z1_i = jnp.repeat(jnp.swapaxes(zi[:, 1:], 1, 2), gc, axis=-1)

    gp = min(g, 8)
    assert g % gp == 0 and gc * gc == n
    blk = lambda *s: pl.BlockSpec((gp,) + s, lambda i: (i, 0, 0))
    k, bm_r, bm_i, cm_r, cm_i = pl.pallas_call(
        _s5_param_kernel,
        grid=(g // gp,),
        in_specs=[blk(ch, p), blk(ch, p), blk(p, n), blk(p, n), blk(p, n), blk(p, n),
                  blk(n, p), blk(n, p), blk(n, p), blk(n, p), blk(p, n), blk(p, n)],
        out_specs=[blk(ch, n), blk(n, p), blk(n, p), blk(p, n), blk(p, n)],
        out_shape=[jax.ShapeDtypeStruct((g, ch, n), F32), jax.ShapeDtypeStruct((g, n, p), F32),
                   jax.ShapeDtypeStruct((g, n, p), F32), jax.ShapeDtypeStruct((g, p, n), F32),
                   jax.ShapeDtypeStruct((g, p, n), F32)],
        compiler_params=_compiler_params(("parallel",)),
        name="s5_params",
    )(zr[:, :ch], zi[:, :ch], bbr_x, bbi_x, cr_x, ci_x, zrev_r, zrev_i, bbt_r, bbt_i, z1_r, z1_i)

    k5 = jnp.concatenate([k.reshape(g, ch, gc, gc), jnp.zeros((g, 1, gc, gc), F32)], axis=1)
    s_idx = jnp.arange(ch)[:, None]
    t_idx = jnp.arange(ch)[None, :]
    lag = jnp.where(t_idx >= s_idx, t_idx - s_idx, ch)
    toep = k5[:, lag]
    toep = toep.transpose(0, 1, 3, 2, 4).reshape(g, n, n).astype(BF16)
    bmat = jnp.concatenate([bm_r, bm_i], axis=-1).astype(BF16)
    cmat = jnp.concatenate([cm_r, cm_i], axis=1).astype(BF16)
    zl_r, zl_i = zr[:, ch], zi[:, ch]
    a1 = jnp.concatenate([zl_r, zl_r], axis=-1)
    a2 = jnp.concatenate([-zl_i, zl_i], axis=-1)
    a2s = jnp.concatenate([zl_i, -zl_i], axis=-1)
    return toep, bmat, cmat, a1, a2, a2s


def _s5_state_kernel(gb, u_ref, bm_ref, e_ref):
    w = bm_ref.shape[-1]
    for gi in range(gb):
        e_ref[:, gi * w:(gi + 1) * w] = jnp.dot(u_ref[gi], bm_ref[gi], preferred_element_type=F32)


def _s5_scan_kernel(e_ref, a1_ref, a2_ref, a2s_ref, x0_ref):
    a1, a2, a2s = a1_ref[...], a2_ref[...], a2s_ref[...]
    half = a1.shape[-1] // 2

    def body(k, carry):
        s, sw = carry
        x0_ref[0, k] = s.astype(BF16)
        e = e_ref[0, k]
        e_sw = pltpu.roll(e, half, axis=1)
        return a1 * s + a2 * sw + e, a1 * sw + a2s * s + e_sw

    zero = jnp.zeros(a1.shape, F32)
    lax.fori_loop(0, e_ref.shape[1], body, (zero, zero), unroll=8)


def _s5_out_kernel(gb, u_ref, toep_ref, x0_ref, cm_ref, d_ref, z_ref):
    w = cm_ref.shape[1]
    for gi in range(gb):
        u = u_ref[gi]
        y = (jnp.dot(u, toep_ref[gi], preferred_element_type=F32)
             + jnp.dot(x0_ref[:, gi * w:(gi + 1) * w], cm_ref[gi], preferred_element_type=F32)
             + d_ref[gi] * u.astype(F32))
        z_ref[gi] = jax.nn.gelu(y).astype(BF16)


def _s5_core(hn, bsz, seq, ssm, d_skip):
    toep, bmat, cmat, a1, a2, a2s = ssm
    t, d = hn.shape
    g = d // SSM_GROUP
    gc, ch = SSM_GROUP, SSM_CHUNK
    n = gc * ch
    nc = seq // ch
    r = bsz * nc
    p2 = bmat.shape[-1]
    u_g = hn.reshape(bsz, nc, ch, g, gc).transpose(3, 0, 1, 2, 4).reshape(g, r, n)

    gb = min(g, 4)
    e = pl.pallas_call(
        functools.partial(_s5_state_kernel, gb),
        grid=(g // gb,),
        in_specs=[pl.BlockSpec((gb, r, n), lambda i: (i, 0, 0)), pl.BlockSpec((gb, n, p2), lambda i: (i, 0, 0))],
        out_specs=pl.BlockSpec((r, gb * p2), lambda i: (0, i)),
        out_shape=jax.ShapeDtypeStruct((r, g * p2), F32),
        compiler_params=_compiler_params(("parallel",)),
        name="s5_chunk_state",
    )(u_g, bmat)

    gs = min(g, 32)
    tab = pl.BlockSpec((gs, p2), lambda b, i: (i, 0))
    x0 = pl.pallas_call(
        _s5_scan_kernel,
        grid=(bsz, g // gs),
        in_specs=[pl.BlockSpec((1, nc, gs, p2), lambda b, i: (b, 0, i, 0)), tab, tab, tab],
        out_specs=pl.BlockSpec((1, nc, gs, p2), lambda b, i: (b, 0, i, 0)),
        out_shape=jax.ShapeDtypeStruct((bsz, nc, g, p2), BF16),
        compiler_params=_compiler_params(("parallel", "parallel")),
        name="s5_chunk_scan",
    )(e.reshape(bsz, nc, g, p2), a1, a2, a2s)

    d_tile = jnp.tile(d_skip.astype(F32).reshape(g, 1, gc), (1, 1, ch))
    z_g = pl.pallas_call(
        functools.partial(_s5_out_kernel, gb),
        grid=(g // gb,),
        in_specs=[pl.BlockSpec((gb, r, n), lambda i: (i, 0, 0)), pl.BlockSpec((gb, n, n), lambda i: (i, 0, 0)),
                  pl.BlockSpec((r, gb * p2), lambda i: (0, i)), pl.BlockSpec((gb, p2, n), lambda i: (i, 0, 0)),
                  pl.BlockSpec((gb, 1, n), lambda i: (i, 0, 0))],
        out_specs=pl.BlockSpec((gb, r, n), lambda i: (i, 0, 0)),
        out_shape=jax.ShapeDtypeStruct((g, r, n), BF16),
        compiler_params=_compiler_params(("parallel",)),
        name="s5_chunk_out",
    )(u_g, toep, x0.reshape(r, g * p2), cmat, d_tile)
    return z_g.reshape(g, bsz, nc, ch, gc).transpose(1, 2, 3, 0, 4).reshape(t, d)


def _glu_kernel(z_ref, h_ref, w1_ref, w2_ref, b_ref, g_ref, o_ref):
    z = z_ref[...]
    lin = jnp.dot(z, w1_ref[...], preferred_element_type=F32)
    gate = jnp.dot(z, w2_ref[...], preferred_element_type=F32) + b_ref[...]
    o_ref[...] = h_ref[...] + _rms_norm(lin * jax.nn.sigmoid(gate), g_ref[...])


def _glu(z, h, w_out, w_gate, b_gate, g_post, tm=512):
    t, d = h.shape
    tm = _row_tile(t, tm)
    row = pl.BlockSpec((tm, d), lambda i: (i, 0))
    return pl.pallas_call(
        _glu_kernel,
        grid=(t // tm,),
        in_specs=[row, row, _resident((d, d)), _resident((d, d)), _resident((1, d)), _resident((1, d))],
        out_specs=row,
        out_shape=jax.ShapeDtypeStruct((t, d), F32),
        compiler_params=_compiler_params(("parallel",)),
        name="s5_glu",
    )(z, h, w_out, w_gate, b_gate.reshape(1, d), g_post.reshape(1, d))


def _rope_tables(seq):
    half = ROPE_DIM // 2
    inv_freq = ROPE_THETA ** (-jnp.arange(half, dtype=F32) / half)
    ang = jnp.arange(seq, dtype=F32)[:, None] * inv_freq[None, :]
    cos, sin = jnp.cos(ang), jnp.sin(ang)
    ones = jnp.ones((seq, HEAD_DIM - ROPE_DIM), F32)
    zeros = jnp.zeros((seq, HEAD_DIM - half), F32)
    cos_h = jnp.concatenate([cos, cos, ones], axis=1)
    sin_up = jnp.concatenate([-sin, zeros], axis=1)
    sin_dn = jnp.concatenate([jnp.zeros((seq, half), F32), sin, jnp.zeros((seq, HEAD_DIM - ROPE_DIM), F32)], axis=1)
    rep = LANES // HEAD_DIM
    return jnp.tile(cos_h, (1, rep)), jnp.tile(sin_up, (1, rep)), jnp.tile(sin_dn, (1, rep))


def _rope_lanes(x, cos, sin_up, sin_dn):
    half = ROPE_DIM // 2
    return (x * cos + pltpu.roll(x, LANES - half, axis=1) * sin_up + pltpu.roll(x, half, axis=1) * sin_dn)


def _kv_kernel(h_ref, g_ref, w_ref, b_ref, cos_ref, sup_ref, sdn_ref, k_ref, v_ref):
    hn = _rms_norm(h_ref[...], g_ref[...]).astype(BF16)
    kv = jnp.dot(hn, w_ref[...], preferred_element_type=F32) + b_ref[...]
    kw = k_ref.shape[-1]
    cos, sup, sdn = cos_ref[...], sup_ref[...], sdn_ref[...]
    for c in range(kw // LANES):
        sl = slice(c * LANES, (c + 1) * LANES)
        k_ref[:, sl] = _rope_lanes(kv[:, sl], cos, sup, sdn).astype(BF16)
    v_ref[...] = kv[:, kw:].astype(BF16)


def _shared_kv(h, seq, g, w_kv, b_kv, rope, tm=512):
    t, d = h.shape
    kw = w_kv.shape[1] // 2
    tm = _row_tile(seq, tm)
    row = pl.BlockSpec((tm, d), lambda i: (i, 0))
    tab = pl.BlockSpec((tm, LANES), lambda i: (i % (seq // tm), 0))
    out = pl.BlockSpec((tm, kw), lambda i: (i, 0))
    return pl.pallas_call(
        _kv_kernel,
        grid=(t // tm,),
        in_specs=[row, _resident((1, d)), _resident((d, 2 * kw)), _resident((1, 2 * kw)), tab, tab, tab],
        out_specs=[out, out],
        out_shape=[jax.ShapeDtypeStruct((t, kw), BF16)] * 2,
        compiler_params=_compiler_params(("parallel",)),
        name="shared_kv",
    )(h, g.reshape(1, d), w_kv, b_kv.reshape(1, 2 * kw), *rope)


def _q_kernel(hn_ref, w_ref, b_ref, cos_ref, sup_ref, sdn_ref, q_ref):
    q = jnp.dot(hn_ref[...], w_ref[...], preferred_element_type=F32) + b_ref[...]
    cos, sup, sdn = cos_ref[...], sup_ref[...], sdn_ref[...]
    scale = HEAD_DIM ** -0.5
    for c in range(q.shape[-1] // LANES):
        sl = slice(c * LANES, (c + 1) * LANES)
        q_ref[:, sl] = (_rope_lanes(q[:, sl], cos, sup, sdn) * scale).astype(BF16)


def _q_proj(hn, seq, w_q, b_q, rope, tm=512):
    t, d = hn.shape
    tm = _row_tile(seq, tm)
    row = pl.BlockSpec((tm, d), lambda i: (i, 0))
    tab = pl.BlockSpec((tm, LANES), lambda i: (i % (seq // tm), 0))
    return pl.pallas_call(
        _q_kernel,
        grid=(t // tm,),
        in_specs=[row, _resident((d, d)), _resident((1, d)), tab, tab, tab],
        out_specs=row,
        out_shape=jax.ShapeDtypeStruct((t, d), BF16),
        compiler_params=_compiler_params(("parallel",)),
        name="q_proj",
    )(hn, w_q, b_q.reshape(1, d), *rope)


def _attn_kernel(n_kv, sink_ref, q_ref, kc_ref, kp_ref, vc_ref, vp_ref, o_ref):
    blk = ATTN_BLOCK
    n = pl.program_id(1)
    qi = lax.broadcasted_iota(jnp.int32, (blk, 2 * blk), 0)
    kj = lax.broadcasted_iota(jnp.int32, (blk, 2 * blk), 1)
    diff = qi + blk - kj
    valid = (diff >= 0) & (diff < blk) & ((n > 0) | (kj >= blk))
    for kvh in range(n_kv):
        ksl = slice(kvh * HEAD_DIM, (kvh + 1) * HEAD_DIM)
        kb = jnp.concatenate([kp_ref[:, ksl], kc_ref[:, ksl]], axis=0)
        vb = jnp.concatenate([vp_ref[:, ksl], vc_ref[:, ksl]], axis=0)
        for gq in range(Q_PER_KV):
            head = kvh * Q_PER_KV + gq
            hsl = slice(head * HEAD_DIM, (head + 1) * HEAD_DIM)
            s = lax.dot_general(q_ref[:, hsl], kb, (((1,), (1,)), ((), ())), preferred_element_type=F32)
            s = jnp.where(valid, s, MASK_VALUE)
            sink = sink_ref[head]
            m = jnp.maximum(jnp.max(s, axis=-1, keepdims=True), sink)
            p = jnp.exp(s - m)
            den = jnp.sum(p, axis=-1, keepdims=True) + jnp.exp(sink - m)
            o = jnp.dot(p.astype(BF16), vb, preferred_element_type=F32) / den
            o_ref[:, hsl] = o.astype(BF16)


def _attention(q, k, v, sinks, bsz, seq):
    t, d = q.shape
    kw = k.shape[1]
    blk = ATTN_BLOCK
    nb = seq // blk
    cur = lambda b, n: (b * nb + n, 0)
    prev = lambda b, n: (b * nb + jnp.maximum(n - 1, 0), 0)
    return pl.pallas_call(
        functools.partial(_attn_kernel, kw // HEAD_DIM),
        grid=(bsz, nb),
        in_specs=[pl.BlockSpec(memory_space=pltpu.SMEM),
                  pl.BlockSpec((blk, d), cur),
                  pl.BlockSpec((blk, kw), cur), pl.BlockSpec((blk, kw), prev),
                  pl.BlockSpec((blk, kw), cur), pl.BlockSpec((blk, kw), prev)],
        out_specs=pl.BlockSpec((blk, d), cur),
        out_shape=jax.ShapeDtypeStruct((t, d), BF16),
        compiler_params=_compiler_params(("parallel", "parallel")),
        name="swa_sink_attention",
    )(sinks.astype(F32), q, k, k, v, v)


def _o_kernel(a_ref, h_ref, w_ref, b_ref, g_ref, o_ref):
    mix = jnp.dot(a_ref[...], w_ref[...], preferred_element_type=F32) + b_ref[...]
    o_ref[...] = h_ref[...] + _rms_norm(mix, g_ref[...])


def _o_proj(a, h, w_o, b_o, g_post, tm=512):
    t, d = h.shape
    tm = _row_tile(t, tm)
    row = pl.BlockSpec((tm, d), lambda i: (i, 0))
    return pl.pallas_call(
        _o_kernel,
        grid=(t // tm,),
        in_specs=[row, row, _resident((d, d)), _resident((1, d)), _resident((1, d))],
        out_specs=row,
        out_shape=jax.ShapeDtypeStruct((t, d), F32),
        compiler_params=_compiler_params(("parallel",)),
        name="o_proj",
    )(a, h, w_o, b_o.reshape(1, d), g_post.reshape(1, d))


def kernel(x, norm_g, ffn_w_gate, ffn_w_up, ffn_w_down, ssm_a_re, ssm_a_im, ssm_log_dt, ssm_b_re, ssm_b_im, ssm_c_re, ssm_c_im, ssm_d, glu_w_out, glu_w_gate, glu_b_gate, kv_norm_g, w_kv, b_kv, w_q, b_q, attn_sinks, w_o, b_o):
    bsz, seq, d = x.shape
    depth = norm_g.shape[0]
    n_a = ssm_a_re.shape[0]
    assert seq % ATTN_BLOCK == 0 and seq % SSM_CHUNK == 0 and d % LANES == 0
    h = x.astype(F32).reshape(bsz * seq, d)
    rope = _rope_tables(seq)
    k = v = None
    for layer in range(depth):
        g = norm_g[layer].astype(F32)
        h, hn = _ffn(h, g[0], g[1], ffn_w_gate[layer, 0].astype(BF16), ffn_w_up[layer, 0].astype(BF16),
                     ffn_w_down[layer, 0].astype(BF16), g_next=g[2])
        if layer < n_a:
            ssm = _s5_params(ssm_a_re[layer], ssm_a_im[layer], ssm_log_dt[layer], ssm_b_re[layer].astype(F32),
                             ssm_b_im[layer].astype(F32), ssm_c_re[layer].astype(F32), ssm_c_im[layer].astype(F32))
            z = _s5_core(hn, bsz, seq, ssm, ssm_d[layer])
            h = _glu(z, h, glu_w_out[layer].astype(BF16), glu_w_gate[layer].astype(BF16), glu_b_gate[layer], g[3])
        else:
            bl = layer - n_a
            q = _q_proj(hn, seq, w_q[bl].astype(BF16), b_q[bl], rope)
            a = _attention(q, k, v, attn_sinks[bl], bsz, seq)
            h = _o_proj(a, h, w_o[bl].astype(BF16), b_o[bl], g[3])
        h = _ffn(h, g[4], g[5], ffn_w_gate[layer, 1].astype(BF16), ffn_w_up[layer, 1].astype(BF16),
                 ffn_w_down[layer, 1].astype(BF16))
        if layer == n_a - 1:
            k, v = _shared_kv(h, seq, kv_norm_g.astype(F32), w_kv.astype(BF16), b_kv, rope)
    return h.reshape(bsz, seq, d).astype(x.dtype)
```

```python
import functools
import math

import jax
import jax.numpy as jnp
from jax import lax
from jax.experimental import pallas as pl
from jax.experimental.pallas import tpu as pltpu

F32 = jnp.float32
BF16 = jnp.bfloat16

NORM_EPS = 1e-6
FFN_RESIDUAL_WEIGHT = 0.5
SSM_GROUP = 16
SSM_CHUNK = 16
HEAD_DIM = 64
Q_PER_KV = 8
ATTN_BLOCK = 128
ROPE_DIM = HEAD_DIM // 4
ROPE_THETA = 500000.0
MASK_VALUE = -1e30
LANES = 128
VMEM_LIMIT_BYTES = 56 * 1024 * 1024


def _compiler_params(semantics):
    return pltpu.CompilerParams(dimension_semantics=semantics, vmem_limit_bytes=VMEM_LIMIT_BYTES)


def _rms_norm(x, g):
    return x * lax.rsqrt(jnp.mean(x * x, axis=-1, keepdims=True) + NORM_EPS) * g


def _resident(shape):
    return pl.BlockSpec(shape, lambda *_: (0,) * len(shape), pipeline_mode=pl.Buffered(1))


def _row_tile(t, want):
    tm = min(t, want)
    assert t % tm == 0
    return tm


def _ffn_kernel(emit_next, h_ref, gpre_ref, gpost_ref, *rest):
    if emit_next:
        gnext_ref, wg_ref, wu_ref, wd_ref, o_ref, on_ref, xn_ref, acc_ref = rest
    else:
        wg_ref, wu_ref, wd_ref, o_ref, xn_ref, acc_ref = rest
    j = pl.program_id(1)

    @pl.when(j == 0)
    def _():
        xn_ref[...] = _rms_norm(h_ref[...], gpre_ref[...]).astype(BF16)
        acc_ref[...] = jnp.zeros_like(acc_ref)

    xn = xn_ref[...]
    gate = jnp.dot(xn, wg_ref[...], preferred_element_type=F32)
    up = jnp.dot(xn, wu_ref[...], preferred_element_type=F32)
    act = (gate * jax.nn.sigmoid(gate) * up).astype(BF16)
    acc_ref[...] += jnp.dot(act, wd_ref[...], preferred_element_type=F32)

    @pl.when(j == pl.num_programs(1) - 1)
    def _():
        h_new = h_ref[...] + FFN_RESIDUAL_WEIGHT * _rms_norm(acc_ref[...], gpost_ref[...])
        o_ref[...] = h_new
        if emit_next:
            on_ref[...] = _rms_norm(h_new, gnext_ref[...]).astype(BF16)


def _ffn(h, g_pre, g_post, w_gate, w_up, w_down, layer, which, g_next=None, tm=512, tf=512):
    t, d = h.shape
    f = w_gate.shape[-1]
    tm = _row_tile(t, tm)
    tf = _row_tile(f, tf)
    emit_next = g_next is not None
    row = pl.BlockSpec((tm, d), lambda i, j: (i, 0))
    vec = pl.BlockSpec((1, d), lambda i, j: (0, 0))
    in_specs = [row, vec, vec] + ([vec] if emit_next else []) + [
        pl.BlockSpec((None, None, d, tf), lambda i, j: (layer, which, 0, j)),
        pl.BlockSpec((None, None, d, tf), lambda i, j: (layer, which, 0, j)),
        pl.BlockSpec((None, None, tf, d), lambda i, j: (layer, which, j, 0)),
    ]
    args = [h, g_pre.reshape(1, d), g_post.reshape(1, d)] + ([g_next.reshape(1, d)] if emit_next else []) + [
        w_gate, w_up, w_down]
    out_shape = [jax.ShapeDtypeStruct((t, d), F32)] + ([jax.ShapeDtypeStruct((t, d), BF16)] if emit_next else [])
    out_specs = [row] + ([row] if emit_next else [])
    outs = pl.pallas_call(
        functools.partial(_ffn_kernel, emit_next),
        grid=(t // tm, f // tf),
        in_specs=in_specs,
        out_specs=out_specs,
        out_shape=out_shape,
        scratch_shapes=[pltpu.VMEM((tm, d), BF16), pltpu.VMEM((tm, d), F32)],
        compiler_params=_compiler_params(("parallel", "arbitrary")),
        name="ffn_next" if emit_next else "ffn",
    )(*args)
    return outs if emit_next else outs[0]


def _s5_param_kernel(zr_ref, zi_ref, bbr_x_ref, bbi_x_ref, cr_x_ref, ci_x_ref,
                     zrev_r_ref, zrev_i_ref, bbt_r_ref, bbt_i_ref, z1_r_ref, z1_i_ref,
                     k_ref, bm_r_ref, bm_i_ref, cm_r_ref, cm_i_ref):
    bbr, bbi, cr, ci = bbr_x_ref[...], bbi_x_ref[...], cr_x_ref[...], ci_x_ref[...]
    w1 = bbr * cr - bbi * ci
    w2 = bbi * cr + bbr * ci
    dn = (((2,), (1,)), ((0,), (0,)))
    k_ref[...] = (lax.dot_general(zr_ref[...], w1, dn, precision=lax.Precision.HIGHEST, preferred_element_type=F32)
                  - lax.dot_general(zi_ref[...], w2, dn, precision=lax.Precision.HIGHEST,
                                    preferred_element_type=F32))
    zvr, zvi, btr, bti = zrev_r_ref[...], zrev_i_ref[...], bbt_r_ref[...], bbt_i_ref[...]
    bm_r_ref[...] = zvr * btr - zvi * bti
    bm_i_ref[...] = zvr * bti + zvi * btr
    z1r, z1i = z1_r_ref[...], z1_i_ref[...]
    cm_r_ref[...] = cr * z1r - ci * z1i
    cm_i_ref[...] = -(cr * z1i + ci * z1r)


def _s5_params(a_re, a_im, log_dt, b_re, b_im, c_re, c_im):
    g, p = a_re.shape
    gc, ch = SSM_GROUP, SSM_CHUNK
    n = ch * gc
    dt = jnp.exp(log_dt.astype(F32))[:, None]
    lam_re = a_re.astype(F32) * dt
    lam_im = a_im.astype(F32) * dt
    lags = jnp.arange(ch + 1, dtype=F32)[None, :, None]
    mag = jnp.exp(lags * lam_re[:, None, :])
    zr = mag * jnp.cos(lags * lam_im[:, None, :])
    zi = mag * jnp.sin(lags * lam_im[:, None, :])
    lb_re, lb_im = zr[:, 1], zi[:, 1]
    den = a_re * a_re + a_im * a_im
    num_re = lb_re - 1.0
    f_re = (num_re * a_re + lb_im * a_im) / den
    f_im = (lb_im * a_re - num_re * a_im) / den
    bb_re = f_re[..., None] * b_re - f_im[..., None] * b_im
    bb_im = f_re[..., None] * b_im + f_im[..., None] * b_re

    bbr_x = jnp.repeat(bb_re, gc, axis=-1)
    bbi_x = jnp.repeat(bb_im, gc, axis=-1)
    cr_x = jnp.tile(jnp.swapaxes(c_re, 1, 2), (1, 1, ch))
    ci_x = jnp.tile(jnp.swapaxes(c_im, 1, 2), (1, 1, ch))
    zrev_r = jnp.repeat(zr[:, ch - 1::-1][:, :ch], gc, axis=1)
    zrev_i = jnp.repeat(zi[:, ch - 1::-1][:, :ch], gc, axis=1)
    bbt_r = jnp.tile(jnp.swapaxes(bb_re, 1, 2), (1, ch, 1))
    bbt_i = jnp.tile(jnp.swapaxes(bb_im, 1, 2), (1, ch, 1))
    z1_r = jnp.repeat(jnp.swapaxes(zr[:, 1:], 1, 2), gc, axis=-1)
    z1_i = jnp.repeat(jnp.swapaxes(zi[:, 1:], 1, 2), gc, axis=-1)

    gp = min(g, 8)
    assert g % gp == 0 and gc * gc == n
    blk = lambda *s: pl.BlockSpec((gp,) + s, lambda i: (i, 0, 0))
    k, bm_r, bm_i, cm_r, cm_i = pl.pallas_call(
        _s5_param_kernel,
        grid=(g // gp,),
        in_specs=[blk(ch, p), blk(ch, p), blk(p, n), blk(p, n), blk(p, n), blk(p, n),
                  blk(n, p), blk(n, p), blk(n, p), blk(n, p), blk(p, n), blk(p, n)],
        out_specs=[blk(ch, n), blk(n, p), blk(n, p), blk(p, n), blk(p, n)],
        out_shape=[jax.ShapeDtypeStruct((g, ch, n), F32), jax.ShapeDtypeStruct((g, n, p), F32),
                   jax.ShapeDtypeStruct((g, n, p), F32), jax.ShapeDtypeStruct((g, p, n), F32),
                   jax.ShapeDtypeStruct((g, p, n), F32)],
        compiler_params=_compiler_params(("parallel",)),
        name="s5_params",
    )(zr[:, :ch], zi[:, :ch], bbr_x, bbi_x, cr_x, ci_x, zrev_r, zrev_i, bbt_r, bbt_i, z1_r, z1_i)

    k5 = jnp.concatenate([k.reshape(g, ch, gc, gc), jnp.zeros((g, 1, gc, gc), F32)], axis=1)
    s_idx = jnp.arange(ch)[:, None]
    t_idx = jnp.arange(ch)[None, :]
    lag = jnp.where(t_idx >= s_idx, t_idx - s_idx, ch)
    toep = k5[:, lag]
    toep = toep.transpose(0, 1, 3, 2, 4).reshape(g, n, n).astype(BF16)
    bmat = jnp.concatenate([bm_r, bm_i], axis=-1).astype(BF16)
    cmat = jnp.concatenate([cm_r, cm_i], axis=1).astype(BF16)
    zl_r, zl_i = zr[:, ch], zi[:, ch]
    a1 = jnp.concatenate([zl_r, zl_r], axis=-1)
    a2 = jnp.concatenate([-zl_i, zl_i], axis=-1)
    a2s = jnp.concatenate([zl_i, -zl_i], axis=-1)
    return toep, bmat, cmat, a1, a2, a2s


GROUPS_PER_TILE = LANES // SSM_GROUP
TILES_PER_CHUNK_ROW = SSM_CHUNK * SSM_GROUP // LANES
RELAYOUT_ROWS = 16


def _lane_block_ids(rows):
    return lax.broadcasted_iota(jnp.int32, (rows, LANES), 1) // SSM_GROUP


def _s5_state_kernel(x_ref, bm_ref, u_ref, e_ref, xf_ref):
    ch, gc, rb = SSM_CHUNK, SSM_GROUP, RELAYOUT_ROWS
    n_chunks = u_ref.shape[1]
    xf_ref[...] = x_ref[...].astype(F32)
    blk_id = _lane_block_ids(rb)

    def body(i, carry):
        row0 = pl.multiple_of(i * rb, rb)
        tok = [xf_ref[pl.ds(row0 * ch + s, rb, stride=ch), :].astype(BF16) for s in range(ch)]
        for g in range(GROUPS_PER_TILE):
            for hf in range(TILES_PER_CHUNK_ROW):
                acc = None
                for j in range(GROUPS_PER_TILE):
                    shift = ((j - g) * gc) % LANES
                    piece = tok[hf * GROUPS_PER_TILE + j]
                    piece = pltpu.roll(piece, shift, axis=1) if shift else piece
                    acc = piece if acc is None else jnp.where(blk_id == j, piece, acc)
                u_ref[g, pl.ds(row0, rb), hf * LANES:(hf + 1) * LANES] = acc
        return carry

    lax.fori_loop(0, n_chunks // rb, body, 0, unroll=4)
    w = bm_ref.shape[-1]
    for g in range(GROUPS_PER_TILE):
        e_ref[:, g * w:(g + 1) * w] = jnp.dot(u_ref[g], bm_ref[g], preferred_element_type=F32)


def _s5_scan_kernel(e_ref, a1_ref, a2_ref, a2s_ref, x0_ref):
    a1, a2, a2s = a1_ref[...], a2_ref[...], a2s_ref[...]
    half = a1.shape[-1] // 2

    def body(k, carry):
        s, sw = carry
        x0_ref[0, k] = s.astype(BF16)
        e = e_ref[0, k]
        e_sw = pltpu.roll(e, half, axis=1)
        return a1 * s + a2 * sw + e, a1 * sw + a2s * s + e_sw

    zero = jnp.zeros(a1.shape, F32)
    lax.fori_loop(0, e_ref.shape[1], body, (zero, zero), unroll=8)


def _s5_out_kernel(u_ref, toep_ref, x0_ref, cm_ref, d_ref, z_ref, y_ref, zf_ref):
    ch, gc, rb = SSM_CHUNK, SSM_GROUP, RELAYOUT_ROWS
    n_chunks = u_ref.shape[1]
    w = cm_ref.shape[1]
    for g in range(GROUPS_PER_TILE):
        u = u_ref[g]
        y = (jnp.dot(u, toep_ref[g], preferred_element_type=F32)
             + jnp.dot(x0_ref[:, g * w:(g + 1) * w], cm_ref[g], preferred_element_type=F32)
             + d_ref[g] * u.astype(F32))
        y_ref[g] = jax.nn.gelu(y).astype(BF16)
    blk_id = _lane_block_ids(rb)

    def body(i, carry):
        row0 = pl.multiple_of(i * rb, rb)
        grp = [[y_ref[g, pl.ds(row0, rb), hf * LANES:(hf + 1) * LANES] for hf in range(TILES_PER_CHUNK_ROW)]
               for g in range(GROUPS_PER_TILE)]
        for s in range(ch):
            hf, j = divmod(s, GROUPS_PER_TILE)
            acc = None
            for g in range(GROUPS_PER_TILE):
                shift = ((g - j) * gc) % LANES
                piece = grp[g][hf]
                piece = pltpu.roll(piece, shift, axis=1) if shift else piece
                acc = piece if acc is None else jnp.where(blk_id == g, piece, acc)
            zf_ref[pl.ds(row0 * ch + s, rb, stride=ch), :] = acc.astype(F32)
        return carry

    lax.fori_loop(0, n_chunks // rb, body, 0, unroll=4)
    z_ref[...] = zf_ref[...].astype(BF16)


def _s5_core(hn, bsz, seq, ssm, d_skip):
    toep, bmat, cmat, a1, a2, a2s = ssm
    t, d = hn.shape
    g = d // SSM_GROUP
    gc, ch = SSM_GROUP, SSM_CHUNK
    n = gc * ch
    nc = seq // ch
    r = bsz * nc
    p2 = bmat.shape[-1]
    gb = GROUPS_PER_TILE
    assert r % RELAYOUT_ROWS == 0 and g % gb == 0

    u_g, e = pl.pallas_call(
        _s5_state_kernel,
        grid=(g // gb,),
        in_specs=[pl.BlockSpec((t, LANES), lambda i: (0, i)), pl.BlockSpec((gb, n, p2), lambda i: (i, 0, 0))],
        out_specs=[pl.BlockSpec((gb, r, n), lambda i: (i, 0, 0)), pl.BlockSpec((r, gb * p2), lambda i: (0, i))],
        out_shape=[jax.ShapeDtypeStruct((g, r, n), BF16), jax.ShapeDtypeStruct((r, g * p2), F32)],
        scratch_shapes=[pltpu.VMEM((t, LANES), F32)],
        compiler_params=_compiler_params(("parallel",)),
        name="s5_chunk_state",
    )(hn, bmat)

    gs = min(g, 32)
    tab = pl.BlockSpec((gs, p2), lambda b, i: (i, 0))
    x0 = pl.pallas_call(
        _s5_scan_kernel,
        grid=(bsz, g // gs),
        in_specs=[pl.BlockSpec((1, nc, gs, p2), lambda b, i: (b, 0, i, 0)), tab, tab, tab],
        out_specs=pl.BlockSpec((1, nc, gs, p2), lambda b, i: (b, 0, i, 0)),
        out_shape=jax.ShapeDtypeStruct((bsz, nc, g, p2), BF16),
        compiler_params=_compiler_params(("parallel", "parallel")),
        name="s5_chunk_scan",
    )(e.reshape(bsz, nc, g, p2), a1, a2, a2s)

    d_tile = jnp.tile(d_skip.astype(F32).reshape(g, 1, gc), (1, 1, ch))
    return pl.pallas_call(
        _s5_out_kernel,
        grid=(g // gb,),
        in_specs=[pl.BlockSpec((gb, r, n), lambda i: (i, 0, 0)), pl.BlockSpec((gb, n, n), lambda i: (i, 0, 0)),
                  pl.BlockSpec((r, gb * p2), lambda i: (0, i)), pl.BlockSpec((gb, p2, n), lambda i: (i, 0, 0)),
                  pl.BlockSpec((gb, 1, n), lambda i: (i, 0, 0))],
        out_specs=pl.BlockSpec((t, LANES), lambda i: (0, i)),
        out_shape=jax.ShapeDtypeStruct((t, d), BF16),
        scratch_shapes=[pltpu.VMEM((gb, r, n), BF16), pltpu.VMEM((t, LANES), F32)],
        compiler_params=_compiler_params(("parallel",)),
        name="s5_chunk_out",
    )(u_g, toep, x0.reshape(r, g * p2), cmat, d_tile)


def _glu_kernel(z_ref, h_ref, w1_ref, w2_ref, b_ref, g_ref, o_ref):
    z = z_ref[...]
    lin = jnp.dot(z, w1_ref[...], preferred_element_type=F32)
    gate = jnp.dot(z, w2_ref[...], preferred_element_type=F32) + b_ref[...]
    o_ref[...] = h_ref[...] + _rms_norm(lin * jax.nn.sigmoid(gate), g_ref[...])


def _glu(z, h, w_out, w_gate, b_gate, g_post, tm=512):
    t, d = h.shape
    tm = _row_tile(t, tm)
    row = pl.BlockSpec((tm, d), lambda i: (i, 0))
    return pl.pallas_call(
        _glu_kernel,
        grid=(t // tm,),
        in_specs=[row, row, _resident((d, d)), _resident((d, d)), _resident((1, d)), _resident((1, d))],
        out_specs=row,
        out_shape=jax.ShapeDtypeStruct((t, d), F32),
        compiler_params=_compiler_params(("parallel",)),
        name="s5_glu",
    )(z, h, w_out, w_gate, b_gate.reshape(1, d), g_post.reshape(1, d))


_NT = (((1,), (1,)), ((), ()))
_TN = (((0,), (0,)), ((), ()))


def _rope_tables(seq):
    half = ROPE_DIM // 2
    inv_freq = ROPE_THETA ** (-jnp.arange(half, dtype=F32) / half)
    ang = jnp.arange(seq, dtype=F32)[:, None] * inv_freq[None, :]
    cos, sin = jnp.cos(ang), jnp.sin(ang)
    ones = jnp.ones((seq, HEAD_DIM - ROPE_DIM), F32)
    zeros = jnp.zeros((seq, HEAD_DIM - half), F32)
    cos_h = jnp.concatenate([cos, cos, ones], axis=1)
    sin_up = jnp.concatenate([-sin, zeros], axis=1)
    sin_dn = jnp.concatenate([jnp.zeros((seq, half), F32), sin, jnp.zeros((seq, HEAD_DIM - ROPE_DIM), F32)], axis=1)
    rep = LANES // HEAD_DIM
    lane_tabs = (jnp.tile(cos_h, (1, rep)), jnp.tile(sin_up, (1, rep)), jnp.tile(sin_dn, (1, rep)))
    return lane_tabs, (cos.T, sin.T)


def _rope_lanes(x, cos, sin_up, sin_dn):
    half = ROPE_DIM // 2
    return (x * cos + pltpu.roll(x, LANES - half, axis=1) * sin_up + pltpu.roll(x, half, axis=1) * sin_dn)


def _kv_kernel(h_ref, g_ref, wk_ref, bk_ref, wvt_ref, bvt_ref, cos_ref, sup_ref, sdn_ref, k_ref, vt_ref):
    hn = _rms_norm(h_ref[...], g_ref[...]).astype(BF16)
    k = jnp.dot(hn, wk_ref[...], preferred_element_type=F32) + bk_ref[...]
    cos, sup, sdn = cos_ref[...], sup_ref[...], sdn_ref[...]
    heads_per_tile = LANES // HEAD_DIM
    for c in range(k.shape[-1] // LANES):
        kr = _rope_lanes(k[:, c * LANES:(c + 1) * LANES], cos, sup, sdn).astype(BF16)
        for e in range(heads_per_tile):
            k_ref[c * heads_per_tile + e] = kr[:, e * HEAD_DIM:(e + 1) * HEAD_DIM]
    vt = lax.dot_general(wvt_ref[...], hn, _NT, preferred_element_type=F32) + bvt_ref[...]
    vt_ref[...] = vt.astype(BF16)


def _shared_kv(h, seq, g, w_kv, b_kv, lane_tabs, tm=512):
    t, d = h.shape
    kw = w_kv.shape[1] // 2
    n_kv = kw // HEAD_DIM
    assert kw % LANES == 0
    tm = _row_tile(seq, tm)
    row = pl.BlockSpec((tm, d), lambda i: (i, 0))
    tab = pl.BlockSpec((tm, LANES), lambda i: (i % (seq // tm), 0))
    wk = w_kv[:, :kw].astype(BF16)
    wvt = w_kv[:, kw:].T.astype(BF16)
    return pl.pallas_call(
        _kv_kernel,
        grid=(t // tm,),
        in_specs=[row, _resident((1, d)), _resident((d, kw)), _resident((1, kw)), _resident((kw, d)),
                  _resident((kw, 1)), tab, tab, tab],
        out_specs=[pl.BlockSpec((n_kv, tm, HEAD_DIM), lambda i: (0, i, 0)), pl.BlockSpec((kw, tm), lambda i: (0, i))],
        out_shape=[jax.ShapeDtypeStruct((n_kv, t, HEAD_DIM), BF16), jax.ShapeDtypeStruct((kw, t), BF16)],
        compiler_params=_compiler_params(("parallel",)),
        name="shared_kv",
    )(h, g.reshape(1, d), wk, b_kv[:kw].reshape(1, kw), wvt, b_kv[kw:].reshape(kw, 1), *lane_tabs)


def _q_kernel(hn_ref, wt_ref, bt_ref, cos_ref, sin_ref, q_ref):
    scale = HEAD_DIM ** -0.5
    qt = (lax.dot_general(wt_ref[...], hn_ref[...], _NT, preferred_element_type=F32) + bt_ref[...]) * scale
    q_ref[...] = qt.astype(BF16)
    cos, sin = cos_ref[...], sin_ref[...]
    half = ROPE_DIM // 2
    for head in range(qt.shape[0] // HEAD_DIM):
        r0 = head * HEAD_DIM
        t1, t2 = qt[r0:r0 + half], qt[r0 + half:r0 + ROPE_DIM]
        rot = jnp.concatenate([t1 * cos - t2 * sin, t2 * cos + t1 * sin], axis=0)
        q_ref[r0:r0 + ROPE_DIM, :] = rot.astype(BF16)


def _q_proj(hn, seq, w_q, b_q, freq_tabs, tm=512):
    t, d = hn.shape
    tm = _row_tile(seq, tm)
    half = ROPE_DIM // 2
    tab = pl.BlockSpec((half, tm), lambda i: (0, i % (seq // tm)))
    return pl.pallas_call(
        _q_kernel,
        grid=(t // tm,),
        in_specs=[pl.BlockSpec((tm, d), lambda i: (i, 0)), _resident((d, d)), _resident((d, 1)), tab, tab],
        out_specs=pl.BlockSpec((d, tm), lambda i: (0, i)),
        out_shape=jax.ShapeDtypeStruct((d, t), BF16),
        compiler_params=_compiler_params(("parallel",)),
        name="q_proj",
    )(hn, w_q.T.astype(BF16), b_q.reshape(d, 1), *freq_tabs)


def _attn_kernel(n_kv, qb, sink_ref, q_ref, kc_ref, kp_ref, vc_ref, vp_ref, o_ref):
    blk = ATTN_BLOCK
    width = Q_PER_KV * blk
    step = pl.program_id(1)
    kj = lax.broadcasted_iota(jnp.int32, (2 * blk, width), 0)
    qi = lax.broadcasted_iota(jnp.int32, (2 * blk, width), 1) % blk
    diff = qi + blk - kj
    band = (diff >= 0) & (diff < blk)
    first = band & ((step > 0) | (kj >= blk))
    for j in range(qb):
        lanes = slice(j * blk, (j + 1) * blk)
        valid = first if j == 0 else band
        for kvh in range(n_kv):
            rows = slice(kvh * HEAD_DIM, (kvh + 1) * HEAD_DIM)
            if j == 0:
                kb = jnp.concatenate([kp_ref[kvh], kc_ref[kvh, :blk]], axis=0)
                vbt = jnp.concatenate([vp_ref[rows, :], vc_ref[rows, :blk]], axis=1)
            else:
                kb = kc_ref[kvh, (j - 1) * blk:(j + 1) * blk]
                vbt = vc_ref[rows, (j - 1) * blk:(j + 1) * blk]
            qcat = jnp.concatenate(
                [q_ref[(kvh * Q_PER_KV + gq) * HEAD_DIM:(kvh * Q_PER_KV + gq + 1) * HEAD_DIM, lanes]
                 for gq in range(Q_PER_KV)], axis=1)
            s = jnp.dot(kb, qcat, preferred_element_type=F32)
            s = jnp.where(valid, s, MASK_VALUE)
            sink = sink_ref[kvh]
            m = jnp.maximum(jnp.max(s, axis=0, keepdims=True), sink)
            p = jnp.exp(s - m)
            den = jnp.sum(p, axis=0, keepdims=True) + jnp.exp(sink - m)
            o = jnp.dot(vbt, p.astype(BF16), preferred_element_type=F32) * (1.0 / den)
            for gq in range(Q_PER_KV):
                r0 = (kvh * Q_PER_KV + gq) * HEAD_DIM
                o_ref[r0:r0 + HEAD_DIM, lanes] = o[:, gq * blk:(gq + 1) * blk].astype(BF16)


def _attention(qt, k, vt, sinks, bsz, seq, qb=4):
    d, t = qt.shape
    n_kv = k.shape[0]
    blk = ATTN_BLOCK
    qb = min(qb, seq // blk)
    assert seq % (qb * blk) == 0
    tq = qb * blk
    steps = seq // tq
    sink_x = jnp.repeat(sinks.astype(F32), blk).reshape(n_kv, 1, Q_PER_KV * blk)
    cur = lambda b, n: b * steps + n
    prev = lambda b, n: b * steps * qb + jnp.maximum(n * qb - 1, 0)
    return pl.pallas_call(
        functools.partial(_attn_kernel, n_kv, qb),
        grid=(bsz, steps),
        in_specs=[_resident((n_kv, 1, Q_PER_KV * blk)),
                  pl.BlockSpec((d, tq), lambda b, n: (0, cur(b, n))),
                  pl.BlockSpec((n_kv, tq, HEAD_DIM), lambda b, n: (0, cur(b, n), 0)),
                  pl.BlockSpec((n_kv, blk, HEAD_DIM), lambda b, n: (0, prev(b, n), 0)),
                  pl.BlockSpec((n_kv * HEAD_DIM, tq), lambda b, n: (0, cur(b, n))),
                  pl.BlockSpec((n_kv * HEAD_DIM, blk), lambda b, n: (0, prev(b, n)))],
        out_specs=pl.BlockSpec((d, tq), lambda b, n: (0, cur(b, n))),
        out_shape=jax.ShapeDtypeStruct((d, t), BF16),
        compiler_params=_compiler_params(("parallel", "parallel")),
        name="swa_sink_attention",
    )(sink_x, qt, k, k, vt, vt)


def _o_kernel(at_ref, h_ref, w_ref, b_ref, g_ref, o_ref):
    mix = lax.dot_general(at_ref[...], w_ref[...], _TN, preferred_element_type=F32) + b_ref[...]
    o_ref[...] = h_ref[...] + _rms_norm(mix, g_ref[...])


def _o_proj(at, h, w_o, b_o, g_post, tm=512):
    t, d = h.shape
    tm = _row_tile(t, tm)
    row = pl.BlockSpec((tm, d), lambda i: (i, 0))
    return pl.pallas_call(
        _o_kernel,
        grid=(t // tm,),
        in_specs=[pl.BlockSpec((d, tm), lambda i: (0, i)), row, _resident((d, d)), _resident((1, d)),
                  _resident((1, d))],
        out_specs=row,
        out_shape=jax.ShapeDtypeStruct((t, d), F32),
        compiler_params=_compiler_params(("parallel",)),
        name="o_proj",
    )(at, h, w_o.astype(BF16), b_o.reshape(1, d), g_post.reshape(1, d))


def kernel(x, norm_g, ffn_w_gate, ffn_w_up, ffn_w_down, ssm_a_re, ssm_a_im, ssm_log_dt, ssm_b_re, ssm_b_im, ssm_c_re, ssm_c_im, ssm_d, glu_w_out, glu_w_gate, glu_b_gate, kv_norm_g, w_kv, b_kv, w_q, b_q, attn_sinks, w_o, b_o):
    bsz, seq, d = x.shape
    depth = norm_g.shape[0]
    n_a = ssm_a_re.shape[0]
    assert seq % ATTN_BLOCK == 0 and seq % SSM_CHUNK == 0 and d % LANES == 0
    h = x.astype(F32).reshape(bsz * seq, d)
    lane_tabs, freq_tabs = _rope_tables(seq)
    wg, wu, wd = ffn_w_gate.astype(BF16), ffn_w_up.astype(BF16), ffn_w_down.astype(BF16)
    k = vt = None
    for layer in range(depth):
        g = norm_g[layer].astype(F32)
        h, hn = _ffn(h, g[0], g[1], wg, wu, wd, layer, 0, g_next=g[2])
        if layer < n_a:
            ssm = _s5_params(ssm_a_re[layer], ssm_a_im[layer], ssm_log_dt[layer], ssm_b_re[layer].astype(F32),
                             ssm_b_im[layer].astype(F32), ssm_c_re[layer].astype(F32), ssm_c_im[layer].astype(F32))
            z = _s5_core(hn, bsz, seq, ssm, ssm_d[layer])
            h = _glu(z, h, glu_w_out[layer].astype(BF16), glu_w_gate[layer].astype(BF16), glu_b_gate[layer], g[3])
        else:
            bl = layer - n_a
            qt = _q_proj(hn, seq, w_q[bl], b_q[bl], freq_tabs)
            at = _attention(qt, k, vt, attn_sinks[bl], bsz, seq)
            h = _o_proj(at, h, w_o[bl], b_o[bl], g[3])
        h = _ffn(h, g[4], g[5], wg, wu, wd, layer, 1)
        if layer == n_a - 1:
            k, vt = _shared_kv(h, seq, kv_norm_g.astype(F32), w_kv, b_kv, lane_tabs)
    return h.reshape(bsz, seq, d).astype(x.dtype)
```

```python
import functools
import math

import jax
import jax.numpy as jnp
from jax import lax
from jax.experimental import pallas as pl
from jax.experimental.pallas import tpu as pltpu

F32 = jnp.float32
BF16 = jnp.bfloat16

NORM_EPS = 1e-6
FFN_RESIDUAL_WEIGHT = 0.5
SSM_GROUP = 16
SSM_CHUNK = 16
HEAD_DIM = 64
Q_PER_KV = 8
ATTN_BLOCK = 128
ROPE_DIM = HEAD_DIM // 4
ROPE_THETA = 500000.0
MASK_VALUE = -1e30
LANES = 128
VMEM_LIMIT_BYTES = 60 * 1024 * 1024


def _compiler_params(semantics):
    return pltpu.CompilerParams(dimension_semantics=semantics, vmem_limit_bytes=VMEM_LIMIT_BYTES)


def _rms_norm(x, g):
    return x * lax.rsqrt(jnp.mean(x * x, axis=-1, keepdims=True) + NORM_EPS) * g


def _resident(shape):
    return pl.BlockSpec(shape, lambda *_: (0,) * len(shape), pipeline_mode=pl.Buffered(1))


def _row_tile(t, want):
    tm = min(t, want)
    assert t % tm == 0
    return tm


def _ffn_kernel(emit_next, h_ref, gpre_ref, gpost_ref, *rest):
    if emit_next:
        gnext_ref, wg_ref, wu_ref, wd_ref, o_ref, on_ref, xn_ref, acc_ref = rest
    else:
        wg_ref, wu_ref, wd_ref, o_ref, xn_ref, acc_ref = rest
    j = pl.program_id(1)

    def swiglu_chunk():
        xn = xn_ref[...]
        gate = jnp.dot(xn, wg_ref[...], preferred_element_type=F32)
        up = jnp.dot(xn, wu_ref[...], preferred_element_type=F32)
        act = (gate * jax.nn.sigmoid(gate) * up).astype(BF16)
        return jnp.dot(act, wd_ref[...], preferred_element_type=F32)

    @pl.when(j == 0)
    def _():
        xn_ref[...] = _rms_norm(h_ref[...], gpre_ref[...]).astype(BF16)
        acc_ref[...] = swiglu_chunk()

    @pl.when(j > 0)
    def _():
        acc_ref[...] += swiglu_chunk()

    @pl.when(j == pl.num_programs(1) - 1)
    def _():
        h_new = h_ref[...] + _rms_norm(acc_ref[...], gpost_ref[...])
        o_ref[...] = h_new
        if emit_next:
            on_ref[...] = _rms_norm(h_new, gnext_ref[...]).astype(BF16)


def _ffn(h, g_pre, g_post, w_gate, w_up, w_down, layer, which, g_next=None, tm=1024, tf=512):
    t, d = h.shape
    f = w_gate.shape[-1]
    tm = _row_tile(t, tm)
    tf = _row_tile(f, tf)
    emit_next = g_next is not None
    row = pl.BlockSpec((tm, d), lambda i, j: (i, 0))
    vec = pl.BlockSpec((1, d), lambda i, j: (0, 0))
    in_specs = [row, vec, vec] + ([vec] if emit_next else []) + [
        pl.BlockSpec((None, None, d, tf), lambda i, j: (layer, which, 0, j)),
        pl.BlockSpec((None, None, d, tf), lambda i, j: (layer, which, 0, j)),
        pl.BlockSpec((None, None, tf, d), lambda i, j: (layer, which, j, 0)),
    ]
    args = [h, g_pre.reshape(1, d), (FFN_RESIDUAL_WEIGHT * g_post).reshape(1, d)] + (
        [g_next.reshape(1, d)] if emit_next else []) + [w_gate, w_up, w_down]
    out_shape = [jax.ShapeDtypeStruct((t, d), F32)] + ([jax.ShapeDtypeStruct((t, d), BF16)] if emit_next else [])
    out_row = pl.BlockSpec((tm, d), lambda i, j: (i, 0), pipeline_mode=pl.Buffered(1))
    out_specs = [out_row] + ([out_row] if emit_next else [])
    outs = pl.pallas_call(
        functools.partial(_ffn_kernel, emit_next),
        grid=(t // tm, f // tf),
        in_specs=in_specs,
        out_specs=out_specs,
        out_shape=out_shape,
        scratch_shapes=[pltpu.VMEM((tm, d), BF16), pltpu.VMEM((tm, d), F32)],
        compiler_params=_compiler_params(("parallel", "arbitrary")),
        name="ffn_next" if emit_next else "ffn",
    )(*args)
    return outs if emit_next else outs[0]


_NT = (((1,), (1,)), ((), ()))
_TN = (((0,), (0,)), ((), ()))


def _s5_param_kernel(z0r_ref, z0i_ref, z1r_ref, z1i_ref, zvr_ref, zvi_ref, cr_ref, ci_ref, bbr_ref, bbi_ref,
                     kt_ref, bm_ref, cmt_ref):
    rows, p = z0r_ref.shape[1], z0r_ref.shape[2]

    def rep(a):
        return jnp.broadcast_to(a[:, None, :], (rows, SSM_GROUP, p)).reshape(rows * SSM_GROUP, p)

    def til(a):
        return jnp.broadcast_to(a[None, :, :], (rows, SSM_GROUP, p)).reshape(rows * SSM_GROUP, p)

    for gi in range(z0r_ref.shape[0]):
        cr, ci = til(cr_ref[gi]), til(ci_ref[gi])
        z0r, z0i = rep(z0r_ref[gi]), rep(z0i_ref[gi])
        zc_r = z0r * cr - z0i * ci
        zc_i = z0r * ci + z0i * cr
        kt_ref[gi] = (
            lax.dot_general(bbr_ref[gi], zc_r, _NT, precision=lax.Precision.HIGHEST, preferred_element_type=F32)
            - lax.dot_general(bbi_ref[gi], zc_i, _NT, precision=lax.Precision.HIGHEST, preferred_element_type=F32))
        z1r, z1i = rep(z1r_ref[gi]), rep(z1i_ref[gi])
        cmt_ref[gi] = jnp.concatenate([z1r * cr - z1i * ci, -(z1r * ci + z1i * cr)], axis=1).astype(BF16)
        zvr, zvi = rep(zvr_ref[gi]), rep(zvi_ref[gi])
        btr, bti = til(bbr_ref[gi]), til(bbi_ref[gi])
        bm_ref[gi] = jnp.concatenate([zvr * btr - zvi * bti, zvr * bti + zvi * btr], axis=1).astype(BF16)


def _s5_params(a_re, a_im, log_dt, b_re, b_im, c_re, c_im):
    g, p = a_re.shape
    gc, ch = SSM_GROUP, SSM_CHUNK
    n = ch * gc
    dt = jnp.exp(log_dt.astype(F32))[:, None]
    lam_re = a_re.astype(F32) * dt
    lam_im = a_im.astype(F32) * dt
    lags = jnp.arange(ch + 1, dtype=F32)[None, :, None]
    mag = jnp.exp(lags * lam_re[:, None, :])
    zr = mag * jnp.cos(lags * lam_im[:, None, :])
    zi = mag * jnp.sin(lags * lam_im[:, None, :])
    lb_re, lb_im = zr[:, 1], zi[:, 1]
    den = a_re * a_re + a_im * a_im
    num_re = lb_re - 1.0
    f_re = (num_re * a_re + lb_im * a_im) / den
    f_im = (lb_im * a_re - num_re * a_im) / den
    bbt_re = f_re[:, None, :] * jnp.swapaxes(b_re, 1, 2) - f_im[:, None, :] * jnp.swapaxes(b_im, 1, 2)
    bbt_im = f_re[:, None, :] * jnp.swapaxes(b_im, 1, 2) + f_im[:, None, :] * jnp.swapaxes(b_re, 1, 2)

    gp = min(g, 8)
    assert g % gp == 0
    blk = lambda *s: pl.BlockSpec((gp,) + s, lambda i: (i, 0, 0))
    kt, bmat, cmat_t = pl.pallas_call(
        _s5_param_kernel,
        grid=(g // gp,),
        in_specs=[blk(ch, p)] * 6 + [blk(gc, p)] * 4,
        out_specs=[blk(gc, n), blk(n, 2 * p), blk(n, 2 * p)],
        out_shape=[jax.ShapeDtypeStruct((g, gc, n), F32), jax.ShapeDtypeStruct((g, n, 2 * p), BF16),
                   jax.ShapeDtypeStruct((g, n, 2 * p), BF16)],
        compiler_params=_compiler_params(("parallel",)),
        name="s5_params",
    )(zr[:, :ch], zi[:, :ch], zr[:, 1:], zi[:, 1:], zr[:, ch - 1::-1][:, :ch], zi[:, ch - 1::-1][:, :ch],
      c_re, c_im, bbt_re, bbt_im)

    kt_pad = jnp.pad(kt, ((0, 0), (0, 0), (n, 0)))
    toep = jnp.stack([kt_pad[:, :, n - s * gc:2 * n - s * gc] for s in range(ch)], axis=1)
    toep = toep.reshape(g, n, n).astype(BF16)
    zl_r, zl_i = zr[:, ch], zi[:, ch]
    a1 = jnp.concatenate([zl_r, zl_r], axis=-1)
    a2 = jnp.concatenate([-zl_i, zl_i], axis=-1)
    a2s = jnp.concatenate([zl_i, -zl_i], axis=-1)
    return toep, bmat, cmat_t, a1, a2, a2s


GROUPS_PER_TILE = LANES // SSM_GROUP
TILES_PER_CHUNK_ROW = SSM_CHUNK * SSM_GROUP // LANES
RELAYOUT_ROWS = 16


def _lane_block_ids(rows):
    return lax.broadcasted_iota(jnp.int32, (rows, LANES), 1) // SSM_GROUP


def _s5_state_kernel(x_ref, bm_ref, u_ref, e_ref, xf_ref):
    ch, gc, rb = SSM_CHUNK, SSM_GROUP, RELAYOUT_ROWS
    n_chunks = u_ref.shape[1]
    xf_ref[...] = x_ref[...].astype(F32)
    blk_id = _lane_block_ids(rb)

    def body(i, carry):
        row0 = pl.multiple_of(i * rb, rb)
        tok = [xf_ref[pl.ds(row0 * ch + s, rb, stride=ch), :].astype(BF16) for s in range(ch)]
        for g in range(GROUPS_PER_TILE):
            for hf in range(TILES_PER_CHUNK_ROW):
                acc = None
                for j in range(GROUPS_PER_TILE):
                    shift = ((j - g) * gc) % LANES
                    piece = tok[hf * GROUPS_PER_TILE + j]
                    piece = pltpu.roll(piece, shift, axis=1) if shift else piece
                    acc = piece if acc is None else jnp.where(blk_id == j, piece, acc)
                u_ref[g, pl.ds(row0, rb), hf * LANES:(hf + 1) * LANES] = acc
        return carry

    lax.fori_loop(0, n_chunks // rb, body, 0, unroll=4)
    nc = e_ref.shape[1]
    for g in range(GROUPS_PER_TILE):
        e = jnp.dot(u_ref[g], bm_ref[g], preferred_element_type=F32)
        for b in range(e_ref.shape[0]):
            e_ref[b, :, g, :] = e[b * nc:(b + 1) * nc]


def _s5_scan_kernel(e_ref, a1_ref, a2_ref, a2s_ref, x0_ref):
    a1, a2, a2s = a1_ref[...], a2_ref[...], a2s_ref[...]
    half = a1.shape[-1] // 2

    def body(k, carry):
        s, sw = carry
        x0_ref[0, k] = s
        e = e_ref[0, k]
        e_sw = pltpu.roll(e, half, axis=1)
        return a1 * s + a2 * sw + e, a1 * sw + a2s * s + e_sw

    zero = jnp.zeros(a1.shape, F32)
    lax.fori_loop(0, e_ref.shape[1], body, (zero, zero), unroll=8)


def _s5_out_kernel(u_ref, toep_ref, x0_ref, cmt_ref, d_ref, z_ref, y_ref, zf_ref):
    ch, gc, rb = SSM_CHUNK, SSM_GROUP, RELAYOUT_ROWS
    n_chunks = u_ref.shape[1]
    for g in range(GROUPS_PER_TILE):
        u = u_ref[g]
        x0 = jnp.concatenate([x0_ref[b, :, g, :] for b in range(x0_ref.shape[0])], axis=0).astype(BF16)
        y = (jnp.dot(u, toep_ref[g], preferred_element_type=F32)
             + lax.dot_general(x0, cmt_ref[g], _NT, preferred_element_type=F32)
             + d_ref[g] * u.astype(F32))
        y_ref[g] = jax.nn.gelu(y).astype(BF16)
    blk_id = _lane_block_ids(rb)

    def body(i, carry):
        row0 = pl.multiple_of(i * rb, rb)
        grp = [[y_ref[g, pl.ds(row0, rb), hf * LANES:(hf + 1) * LANES] for hf in range(TILES_PER_CHUNK_ROW)]
               for g in range(GROUPS_PER_TILE)]
        for s in range(ch):
            hf, j = divmod(s, GROUPS_PER_TILE)
            acc = None
            for g in range(GROUPS_PER_TILE):
                shift = ((g - j) * gc) % LANES
                piece = grp[g][hf]
                piece = pltpu.roll(piece, shift, axis=1) if shift else piece
                acc = piece if acc is None else jnp.where(blk_id == g, piece, acc)
            zf_ref[pl.ds(row0 * ch + s, rb, stride=ch), :] = acc.astype(F32)
        return carry

    lax.fori_loop(0, n_chunks // rb, body, 0, unroll=4)
    z_ref[...] = zf_ref[...].astype(BF16)


def _s5_core(hn, bsz, seq, ssm, d_skip):
    toep, bmat, cmat, a1, a2, a2s = ssm
    t, d = hn.shape
    g = d // SSM_GROUP
    gc, ch = SSM_GROUP, SSM_CHUNK
    n = gc * ch
    nc = seq // ch
    r = bsz * nc
    p2 = bmat.shape[-1]
    gb = GROUPS_PER_TILE
    assert r % RELAYOUT_ROWS == 0 and g % gb == 0

    u_g, e = pl.pallas_call(
        _s5_state_kernel,
        grid=(g // gb,),
        in_specs=[pl.BlockSpec((t, LANES), lambda i: (0, i)), pl.BlockSpec((gb, n, p2), lambda i: (i, 0, 0))],
        out_specs=[pl.BlockSpec((gb, r, n), lambda i: (i, 0, 0)),
                   pl.BlockSpec((bsz, nc, gb, p2), lambda i: (0, 0, i, 0))],
        out_shape=[jax.ShapeDtypeStruct((g, r, n), BF16), jax.ShapeDtypeStruct((bsz, nc, g, p2), F32)],
        scratch_shapes=[pltpu.VMEM((t, LANES), F32)],
        compiler_params=_compiler_params(("parallel",)),
        name="s5_chunk_state",
    )(hn, bmat)

    gs = min(g, 32)
    tab = pl.BlockSpec((gs, p2), lambda b, i: (i, 0))
    x0 = pl.pallas_call(
        _s5_scan_kernel,
        grid=(bsz, g // gs),
        in_specs=[pl.BlockSpec((1, nc, gs, p2), lambda b, i: (b, 0, i, 0)), tab, tab, tab],
        out_specs=pl.BlockSpec((1, nc, gs, p2), lambda b, i: (b, 0, i, 0)),
        out_shape=jax.ShapeDtypeStruct((bsz, nc, g, p2), F32),
        compiler_params=_compiler_params(("parallel", "parallel")),
        name="s5_chunk_scan",
    )(e, a1, a2, a2s)

    d_tile = jnp.tile(d_skip.astype(F32).reshape(g, 1, gc), (1, 1, ch))
    return pl.pallas_call(
        _s5_out_kernel,
        grid=(g // gb,),
        in_specs=[pl.BlockSpec((gb, r, n), lambda i: (i, 0, 0)), pl.BlockSpec((gb, n, n), lambda i: (i, 0, 0)),
                  pl.BlockSpec((bsz, nc, gb, p2), lambda i: (0, 0, i, 0)),
                  pl.BlockSpec((gb, n, p2), lambda i: (i, 0, 0)), pl.BlockSpec((gb, 1, n), lambda i: (i, 0, 0))],
        out_specs=pl.BlockSpec((t, LANES), lambda i: (0, i)),
        out_shape=jax.ShapeDtypeStruct((t, d), BF16),
        scratch_shapes=[pltpu.VMEM((gb, r, n), BF16), pltpu.VMEM((t, LANES), F32)],
        compiler_params=_compiler_params(("parallel",)),
        name="s5_chunk_out",
    )(u_g, toep, x0, cmat, d_tile)


def _glu_kernel(z_ref, h_ref, w1_ref, w2_ref, b_ref, g_ref, o_ref):
    z = z_ref[...]
    lin = jnp.dot(z, w1_ref[...], preferred_element_type=F32)
    gate = jnp.dot(z, w2_ref[...], preferred_element_type=F32) + b_ref[...]
    o_ref[...] = h_ref[...] + _rms_norm(lin * jax.nn.sigmoid(gate), g_ref[...])


def _glu(z, h, w_out, w_gate, b_gate, g_post, tm=512):
    t, d = h.shape
    tm = _row_tile(t, tm)
    row = pl.BlockSpec((tm, d), lambda i: (i, 0))
    return pl.pallas_call(
        _glu_kernel,
        grid=(t // tm,),
        in_specs=[row, row, _resident((d, d)), _resident((d, d)), _resident((1, d)), _resident((1, d))],
        out_specs=row,
        out_shape=jax.ShapeDtypeStruct((t, d), F32),
        compiler_params=_compiler_params(("parallel",)),
        name="s5_glu",
    )(z, h, w_out, w_gate, b_gate.reshape(1, d), g_post.reshape(1, d))


def _rope_tables(seq):
    half = ROPE_DIM // 2
    inv_freq = ROPE_THETA ** (-jnp.arange(half, dtype=F32) / half)
    ang = jnp.arange(seq, dtype=F32)[:, None] * inv_freq[None, :]
    cos, sin = jnp.cos(ang), jnp.sin(ang)
    ones = jnp.ones((seq, HEAD_DIM - ROPE_DIM), F32)
    zeros = jnp.zeros((seq, HEAD_DIM - half), F32)
    cos_h = jnp.concatenate([cos, cos, ones], axis=1)
    sin_up = jnp.concatenate([-sin, zeros], axis=1)
    sin_dn = jnp.concatenate([jnp.zeros((seq, half), F32), sin, jnp.zeros((seq, HEAD_DIM - ROPE_DIM), F32)], axis=1)
    rep = LANES // HEAD_DIM
    lane_tabs = (jnp.tile(cos_h, (1, rep)), jnp.tile(sin_up, (1, rep)), jnp.tile(sin_dn, (1, rep)))
    return lane_tabs, (cos.T, sin.T)


def _rope_lanes(x, cos, sin_up, sin_dn):
    half = ROPE_DIM // 2
    return (x * cos + pltpu.roll(x, LANES - half, axis=1) * sin_up + pltpu.roll(x, half, axis=1) * sin_dn)


def _kv_kernel(h_ref, g_ref, wk_ref, bk_ref, wvt_ref, bvt_ref, cos_ref, sup_ref, sdn_ref, k_ref, vt_ref):
    hn = _rms_norm(h_ref[...], g_ref[...]).astype(BF16)
    k = jnp.dot(hn, wk_ref[...], preferred_element_type=F32) + bk_ref[...]
    cos, sup, sdn = cos_ref[...], sup_ref[...], sdn_ref[...]
    heads_per_tile = LANES // HEAD_DIM
    for c in range(k.shape[-1] // LANES):
        kr = _rope_lanes(k[:, c * LANES:(c + 1) * LANES], cos, sup, sdn).astype(BF16)
        for e in range(heads_per_tile):
            k_ref[c * heads_per_tile + e] = kr[:, e * HEAD_DIM:(e + 1) * HEAD_DIM]
    vt = lax.dot_general(wvt_ref[...], hn, _NT, preferred_element_type=F32) + bvt_ref[...]
    vt_ref[...] = vt.astype(BF16)


def _shared_kv(h, seq, g, w_kv, b_kv, lane_tabs, tm=512):
    t, d = h.shape
    kw = w_kv.shape[1] // 2
    n_kv = kw // HEAD_DIM
    assert kw % LANES == 0
    tm = _row_tile(seq, tm)
    row = pl.BlockSpec((tm, d), lambda i: (i, 0))
    tab = pl.BlockSpec((tm, LANES), lambda i: (i % (seq // tm), 0))
    wk = w_kv[:, :kw].astype(BF16)
    wvt = w_kv[:, kw:].T.astype(BF16)
    return pl.pallas_call(
        _kv_kernel,
        grid=(t // tm,),
        in_specs=[row, _resident((1, d)), _resident((d, kw)), _resident((1, kw)), _resident((kw, d)),
                  _resident((kw, 1)), tab, tab, tab],
        out_specs=[pl.BlockSpec((n_kv, tm, HEAD_DIM), lambda i: (0, i, 0)), pl.BlockSpec((kw, tm), lambda i: (0, i))],
        out_shape=[jax.ShapeDtypeStruct((n_kv, t, HEAD_DIM), BF16), jax.ShapeDtypeStruct((kw, t), BF16)],
        compiler_params=_compiler_params(("parallel",)),
        name="shared_kv",
    )(h, g.reshape(1, d), wk, b_kv[:kw].reshape(1, kw), wvt, b_kv[kw:].reshape(kw, 1), *lane_tabs)


def _q_kernel(hn_ref, wt_ref, bt_ref, cos_ref, sin_ref, q_ref):
    scale = HEAD_DIM ** -0.5
    qt = (lax.dot_general(wt_ref[...], hn_ref[...], _NT, preferred_element_type=F32) + bt_ref[...]) * scale
    q_ref[...] = qt.astype(BF16)
    cos, sin = cos_ref[...], sin_ref[...]
    half = ROPE_DIM // 2
    for head in range(qt.shape[0] // HEAD_DIM):
        r0 = head * HEAD_DIM
        t1, t2 = qt[r0:r0 + half], qt[r0 + half:r0 + ROPE_DIM]
        rot = jnp.concatenate([t1 * cos - t2 * sin, t2 * cos + t1 * sin], axis=0)
        q_ref[r0:r0 + ROPE_DIM, :] = rot.astype(BF16)


def _q_proj(hn, seq, w_q, b_q, freq_tabs, tm=512):
    t, d = hn.shape
    tm = _row_tile(seq, tm)
    half = ROPE_DIM // 2
    tab = pl.BlockSpec((half, tm), lambda i: (0, i % (seq // tm)))
    return pl.pallas_call(
        _q_kernel,
        grid=(t // tm,),
        in_specs=[pl.BlockSpec((tm, d), lambda i: (i, 0)), _resident((d, d)), _resident((d, 1)), tab, tab],
        out_specs=pl.BlockSpec((d, tm), lambda i: (0, i)),
        out_shape=jax.ShapeDtypeStruct((d, t), BF16),
        compiler_params=_compiler_params(("parallel",)),
        name="q_proj",
    )(hn, w_q.T.astype(BF16), b_q.reshape(d, 1), *freq_tabs)


def _attn_kernel(n_kv, qb, sink_ref, q_ref, kc_ref, kp_ref, vc_ref, vp_ref, o_ref):
    blk = ATTN_BLOCK
    width = Q_PER_KV * blk
    step = pl.program_id(1)
    kj = lax.broadcasted_iota(jnp.int32, (2 * blk, width), 0)
    qi = lax.broadcasted_iota(jnp.int32, (2 * blk, width), 1) % blk
    diff = qi + blk - kj
    band = (diff >= 0) & (diff < blk)
    first = band & ((step > 0) | (kj >= blk))
    for j in range(qb):
        lanes = slice(j * blk, (j + 1) * blk)
        valid = first if j == 0 else band
        for kvh in range(n_kv):
            rows = slice(kvh * HEAD_DIM, (kvh + 1) * HEAD_DIM)
            if j == 0:
                kb = jnp.concatenate([kp_ref[kvh], kc_ref[kvh, :blk]], axis=0)
                vbt = jnp.concatenate([vp_ref[rows, :], vc_ref[rows, :blk]], axis=1)
            else:
                kb = kc_ref[kvh, (j - 1) * blk:(j + 1) * blk]
                vbt = vc_ref[rows, (j - 1) * blk:(j + 1) * blk]
            qcat = jnp.concatenate(
                [q_ref[(kvh * Q_PER_KV + gq) * HEAD_DIM:(kvh * Q_PER_KV + gq + 1) * HEAD_DIM, lanes]
                 for gq in range(Q_PER_KV)], axis=1)
            s = jnp.dot(kb, qcat, preferred_element_type=F32)
            s = jnp.where(valid, s, MASK_VALUE)
            sink = sink_ref[kvh]
            m = jnp.maximum(jnp.max(s, axis=0, keepdims=True), sink)
            p = jnp.exp(s - m)
            den = jnp.sum(p, axis=0, keepdims=True) + jnp.exp(sink - m)
            o = jnp.dot(vbt, p.astype(BF16), preferred_element_type=F32) * (1.0 / den)
            for gq in range(Q_PER_KV):
                r0 = (kvh * Q_PER_KV + gq) * HEAD_DIM
                o_ref[r0:r0 + HEAD_DIM, lanes] = o[:, gq * blk:(gq + 1) * blk].astype(BF16)


def _attention(qt, k, vt, sinks, bsz, seq, qb=4):
    d, t = qt.shape
    n_kv = k.shape[0]
    blk = ATTN_BLOCK
    qb = min(qb, seq // blk)
    assert seq % (qb * blk) == 0
    tq = qb * blk
    steps = seq // tq
    sink_x = jnp.repeat(sinks.astype(F32), blk).reshape(n_kv, 1, Q_PER_KV * blk)
    cur = lambda b, n: b * steps + n
    prev = lambda b, n: b * steps * qb + jnp.maximum(n * qb - 1, 0)
    return pl.pallas_call(
        functools.partial(_attn_kernel, n_kv, qb),
        grid=(bsz, steps),
        in_specs=[_resident((n_kv, 1, Q_PER_KV * blk)),
                  pl.BlockSpec((d, tq), lambda b, n: (0, cur(b, n))),
                  pl.BlockSpec((n_kv, tq, HEAD_DIM), lambda b, n: (0, cur(b, n), 0)),
                  pl.BlockSpec((n_kv, blk, HEAD_DIM), lambda b, n: (0, prev(b, n), 0)),
                  pl.BlockSpec((n_kv * HEAD_DIM, tq), lambda b, n: (0, cur(b, n))),
                  pl.BlockSpec((n_kv * HEAD_DIM, blk), lambda b, n: (0, prev(b, n)))],
        out_specs=pl.BlockSpec((d, tq), lambda b, n: (0, cur(b, n))),
        out_shape=jax.ShapeDtypeStruct((d, t), BF16),
        compiler_params=_compiler_params(("parallel", "parallel")),
        name="swa_sink_attention",
    )(sink_x, qt, k, k, vt, vt)


def _o_kernel(at_ref, h_ref, w_ref, b_ref, g_ref, o_ref):
    mix = lax.dot_general(at_ref[...], w_ref[...], _TN, preferred_element_type=F32) + b_ref[...]
    o_ref[...] = h_ref[...] + _rms_norm(mix, g_ref[...])


def _o_proj(at, h, w_o, b_o, g_post, tm=512):
    t, d = h.shape
    tm = _row_tile(t, tm)
    row = pl.BlockSpec((tm, d), lambda i: (i, 0))
    return pl.pallas_call(
        _o_kernel,
        grid=(t // tm,),
        in_specs=[pl.BlockSpec((d, tm), lambda i: (0, i)), row, _resident((d, d)), _resident((1, d)),
                  _resident((1, d))],
        out_specs=row,
        out_shape=jax.ShapeDtypeStruct((t, d), F32),
        compiler_params=_compiler_params(("parallel",)),
        name="o_proj",
    )(at, h, w_o.astype(BF16), b_o.reshape(1, d), g_post.reshape(1, d))


def kernel(x, norm_g, ffn_w_gate, ffn_w_up, ffn_w_down, ssm_a_re, ssm_a_im, ssm_log_dt, ssm_b_re, ssm_b_im, ssm_c_re, ssm_c_im, ssm_d, glu_w_out, glu_w_gate, glu_b_gate, kv_norm_g, w_kv, b_kv, w_q, b_q, attn_sinks, w_o, b_o):
    bsz, seq, d = x.shape
    depth = norm_g.shape[0]
    n_a = ssm_a_re.shape[0]
    assert seq % ATTN_BLOCK == 0 and seq % SSM_CHUNK == 0 and d % LANES == 0
    h = x.astype(F32).reshape(bsz * seq, d)
    lane_tabs, freq_tabs = _rope_tables(seq)
    wg, wu, wd = ffn_w_gate.astype(BF16), ffn_w_up.astype(BF16), ffn_w_down.astype(BF16)
    k = vt = None
    for layer in range(depth):
        g = norm_g[layer].astype(F32)
        h, hn = _ffn(h, g[0], g[1], wg, wu, wd, layer, 0, g_next=g[2])
        if layer < n_a:
            ssm = _s5_params(ssm_a_re[layer], ssm_a_im[layer], ssm_log_dt[layer], ssm_b_re[layer].astype(F32),
                             ssm_b_im[layer].astype(F32), ssm_c_re[layer].astype(F32), ssm_c_im[layer].astype(F32))
            z = _s5_core(hn, bsz, seq, ssm, ssm_d[layer])
            h = _glu(z, h, glu_w_out[layer].astype(BF16), glu_w_gate[layer].astype(BF16), glu_b_gate[layer], g[3])
        else:
            bl = layer - n_a
            qt = _q_proj(hn, seq, w_q[bl], b_q[bl], freq_tabs)
            at = _attention(qt, k, vt, attn_sinks[bl], bsz, seq)
            h = _o_proj(at, h, w_o[bl], b_o[bl], g[3])
        h = _ffn(h, g[4], g[5], wg, wu, wd, layer, 1)
        if layer == n_a - 1:
            k, vt = _shared_kv(h, seq, kv_norm_g.astype(F32), w_kv, b_kv, lane_tabs)
    return h.reshape(bsz, seq, d).astype(x.dtype)
```

```python
import functools
import math

import jax
import jax.numpy as jnp
from jax import lax
from jax.experimental import pallas as pl
from jax.experimental.pallas import tpu as pltpu

F32 = jnp.float32
BF16 = jnp.bfloat16

NORM_EPS = 1e-6
FFN_RESIDUAL_WEIGHT = 0.5
SSM_GROUP = 16
SSM_CHUNK = 16
HEAD_DIM = 64
Q_PER_KV = 8
ATTN_BLOCK = 128
ROPE_DIM = HEAD_DIM // 4
ROPE_THETA = 500000.0
MASK_VALUE = -1e30
LOG2_E = math.log2(math.e)
LANES = 128
VMEM_LIMIT_BYTES = 63 * 1024 * 1024


def _compiler_params(semantics):
    return pltpu.CompilerParams(dimension_semantics=semantics, vmem_limit_bytes=VMEM_LIMIT_BYTES)


def _rms_norm(x, g):
    return x * lax.rsqrt(jnp.mean(x * x, axis=-1, keepdims=True) + NORM_EPS) * g


def _resident(shape):
    return pl.BlockSpec(shape, lambda *_: (0,) * len(shape), pipeline_mode=pl.Buffered(1))


def _row_tile(t, want):
    tm = min(t, want)
    assert t % tm == 0
    return tm


def _ffn_kernel(emit_next, cast_next, h_ref, gpre_ref, gpost_ref, *rest):
    rest = list(rest)
    gnext_ref = rest.pop(0) if emit_next else None
    wg_ref, wu_ref, wd_ref = rest[:3]
    rest = rest[3:]
    next_f32 = [rest.pop(0) for _ in range(3)] if cast_next else []
    o_ref = rest.pop(0)
    on_ref = rest.pop(0) if emit_next else None
    next_bf16 = [rest.pop(0) for _ in range(3)] if cast_next else []
    xn_ref, acc_ref = rest
    j = pl.program_id(1)

    for src, dst in zip(next_f32, next_bf16):
        dst[...] = src[...].astype(BF16)

    def swiglu_chunk():
        xn = xn_ref[...]
        gate = jnp.dot(xn, wg_ref[...], preferred_element_type=F32)
        up = jnp.dot(xn, wu_ref[...], preferred_element_type=F32)
        act = (gate * jax.nn.sigmoid(gate) * up).astype(BF16)
        return jnp.dot(act, wd_ref[...], preferred_element_type=F32)

    @pl.when(j == 0)
    def _():
        xn_ref[...] = _rms_norm(h_ref[...], gpre_ref[...]).astype(BF16)
        acc_ref[...] = swiglu_chunk()

    @pl.when(j > 0)
    def _():
        acc_ref[...] += swiglu_chunk()

    @pl.when(j == pl.num_programs(1) - 1)
    def _():
        h_new = h_ref[...] + _rms_norm(acc_ref[...], gpost_ref[...])
        o_ref[...] = h_new
        if emit_next:
            on_ref[...] = _rms_norm(h_new, gnext_ref[...]).astype(BF16)


def _ffn(h, g_pre, g_post, weights, g_next=None, next_weights=None, tm=1024, tf=512):
    t, d = h.shape
    w_gate, w_up, w_down = weights
    f = w_gate.shape[-1]
    tm = _row_tile(t, tm)
    tf = _row_tile(f, tf)
    ni, nj = t // tm, f // tf
    emit_next = g_next is not None
    cast_next = next_weights is not None
    row = pl.BlockSpec((tm, d), lambda i, j: (i, 0))
    vec = pl.BlockSpec((1, d), lambda i, j: (0, 0))
    in_specs = [row, vec, vec] + ([vec] if emit_next else []) + [
        pl.BlockSpec((d, tf), lambda i, j: (0, j)),
        pl.BlockSpec((d, tf), lambda i, j: (0, j)),
        pl.BlockSpec((tf, d), lambda i, j: (j, 0)),
    ]
    args = [h, g_pre.reshape(1, d), (FFN_RESIDUAL_WEIGHT * g_post).reshape(1, d)] + (
        [g_next.reshape(1, d)] if emit_next else []) + [w_gate, w_up, w_down]
    out_shape = [jax.ShapeDtypeStruct((t, d), F32)] + ([jax.ShapeDtypeStruct((t, d), BF16)] if emit_next else [])
    out_row = pl.BlockSpec((tm, d), lambda i, j: (i, 0), pipeline_mode=pl.Buffered(1))
    out_specs = [out_row] + ([out_row] if emit_next else [])
    if cast_next:
        nwg, nwu, nwd, layer, which = next_weights
        dr = d // ni
        assert d % ni == 0 and dr % LANES == 0
        in_specs += [pl.BlockSpec((None, None, dr, tf), lambda i, j: (layer, which, i, j)),
                     pl.BlockSpec((None, None, dr, tf), lambda i, j: (layer, which, i, j)),
                     pl.BlockSpec((None, None, tf, dr), lambda i, j: (layer, which, j, i))]
        args += [nwg, nwu, nwd]
        out_specs += [pl.BlockSpec((dr, tf), lambda i, j: (i, j)), pl.BlockSpec((dr, tf), lambda i, j: (i, j)),
                      pl.BlockSpec((tf, dr), lambda i, j: (j, i))]
        out_shape += [jax.ShapeDtypeStruct((d, f), BF16), jax.ShapeDtypeStruct((d, f), BF16),
                      jax.ShapeDtypeStruct((f, d), BF16)]
    outs = pl.pallas_call(
        functools.partial(_ffn_kernel, emit_next, cast_next),
        grid=(ni, nj),
        in_specs=in_specs,
        out_specs=out_specs,
        out_shape=out_shape,
        scratch_shapes=[pltpu.VMEM((tm, d), BF16), pltpu.VMEM((tm, d), F32)],
        compiler_params=_compiler_params(("parallel", "arbitrary")),
        name="ffn",
    )(*args)
    h_new = outs[0]
    normed = outs[1] if emit_next else None
    casted = tuple(outs[-3:]) if cast_next else None
    return h_new, normed, casted


_NT = (((1,), (1,)), ((), ()))
_TN = (((0,), (0,)), ((), ()))


def _s5_param_kernel(z0r_ref, z0i_ref, z1r_ref, z1i_ref, zvr_ref, zvi_ref, cr_ref, ci_ref, bbr_ref, bbi_ref,
                     kt_ref, bm_ref, cmt_ref):
    rows, p = z0r_ref.shape[1], z0r_ref.shape[2]

    def rep(a):
        return jnp.broadcast_to(a[:, None, :], (rows, SSM_GROUP, p)).reshape(rows * SSM_GROUP, p)

    def til(a):
        return jnp.broadcast_to(a[None, :, :], (rows, SSM_GROUP, p)).reshape(rows * SSM_GROUP, p)

    for gi in range(z0r_ref.shape[0]):
        cr, ci = til(cr_ref[gi]), til(ci_ref[gi])
        z0r, z0i = rep(z0r_ref[gi]), rep(z0i_ref[gi])
        zc_r = z0r * cr - z0i * ci
        zc_i = z0r * ci + z0i * cr
        kt_ref[gi] = (
            lax.dot_general(bbr_ref[gi], zc_r, _NT, precision=lax.Precision.HIGHEST, preferred_element_type=F32)
            - lax.dot_general(bbi_ref[gi], zc_i, _NT, precision=lax.Precision.HIGHEST, preferred_element_type=F32))
        z1r, z1i = rep(z1r_ref[gi]), rep(z1i_ref[gi])
        cmt_ref[gi] = jnp.concatenate([z1r * cr - z1i * ci, -(z1r * ci + z1i * cr)], axis=1).astype(BF16)
        zvr, zvi = rep(zvr_ref[gi]), rep(zvi_ref[gi])
        btr, bti = til(bbr_ref[gi]), til(bbi_ref[gi])
        bm_ref[gi] = jnp.concatenate([zvr * btr - zvi * bti, zvr * bti + zvi * btr], axis=1).astype(BF16)


def _s5_params(a_re, a_im, log_dt, b_re, b_im, c_re, c_im):
    g, p = a_re.shape
    gc, ch = SSM_GROUP, SSM_CHUNK
    n = ch * gc
    dt = jnp.exp(log_dt.astype(F32))[:, None]
    lam_re = a_re.astype(F32) * dt
    lam_im = a_im.astype(F32) * dt
    lags = jnp.arange(ch + 1, dtype=F32)[None, :, None]
    mag = jnp.exp(lags * lam_re[:, None, :])
    zr = mag * jnp.cos(lags * lam_im[:, None, :])
    zi = mag * jnp.sin(lags * lam_im[:, None, :])
    lb_re, lb_im = zr[:, 1], zi[:, 1]
    den = a_re * a_re + a_im * a_im
    num_re = lb_re - 1.0
    f_re = (num_re * a_re + lb_im * a_im) / den
    f_im = (lb_im * a_re - num_re * a_im) / den
    bbt_re = f_re[:, None, :] * jnp.swapaxes(b_re, 1, 2) - f_im[:, None, :] * jnp.swapaxes(b_im, 1, 2)
    bbt_im = f_re[:, None, :] * jnp.swapaxes(b_im, 1, 2) + f_im[:, None, :] * jnp.swapaxes(b_re, 1, 2)

    gp = min(g, 8)
    assert g % gp == 0
    blk = lambda *s: pl.BlockSpec((gp,) + s, lambda i: (i, 0, 0))
    kt, bmat, cmat_t = pl.pallas_call(
        _s5_param_kernel,
        grid=(g // gp,),
        in_specs=[blk(ch, p)] * 6 + [blk(gc, p)] * 4,
        out_specs=[blk(gc, n), blk(n, 2 * p), blk(n, 2 * p)],
        out_shape=[jax.ShapeDtypeStruct((g, gc, n), F32), jax.ShapeDtypeStruct((g, n, 2 * p), BF16),
                   jax.ShapeDtypeStruct((g, n, 2 * p), BF16)],
        compiler_params=_compiler_params(("parallel",)),
        name="s5_params",
    )(zr[:, :ch], zi[:, :ch], zr[:, 1:], zi[:, 1:], zr[:, ch - 1::-1][:, :ch], zi[:, ch - 1::-1][:, :ch],
      c_re, c_im, bbt_re, bbt_im)

    kt_pad = jnp.pad(kt, ((0, 0), (0, 0), (n, 0)))
    toep = jnp.stack([kt_pad[:, :, n - s * gc:2 * n - s * gc] for s in range(ch)], axis=1)
    toep = toep.reshape(g, n, n).astype(BF16)
    zl_r, zl_i = zr[:, ch], zi[:, ch]
    a1 = jnp.concatenate([zl_r, zl_r], axis=-1)
    a2 = jnp.concatenate([-zl_i, zl_i], axis=-1)
    a2s = jnp.concatenate([zl_i, -zl_i], axis=-1)
    return toep, bmat, cmat_t, a1, a2, a2s


GROUPS_PER_TILE = LANES // SSM_GROUP
TILES_PER_CHUNK_ROW = SSM_CHUNK * SSM_GROUP // LANES
RELAYOUT_ROWS = 16


def _lane_block_ids(rows):
    return lax.broadcasted_iota(jnp.int32, (rows, LANES), 1) // SSM_GROUP


def _transpose_granules(v, blk_id):
    v = list(v)
    d = GROUPS_PER_TILE // 2
    while d >= 1:
        low = (blk_id & d) == 0
        nxt = list(v)
        for j in range(GROUPS_PER_TILE):
            if j & d:
                continue
            a, b = v[j], v[j + d]
            nxt[j] = jnp.where(low, a, pltpu.roll(b, d * SSM_GROUP, axis=1))
            nxt[j + d] = jnp.where(low, pltpu.roll(a, LANES - d * SSM_GROUP, axis=1), b)
        v = nxt
        d //= 2
    return v


def _s5_state_kernel(x_ref, bm_ref, u_ref, e_ref, xf_ref):
    ch, gc, rb = SSM_CHUNK, SSM_GROUP, RELAYOUT_ROWS
    n_chunks = u_ref.shape[1]
    xf_ref[...] = x_ref[...].astype(F32)
    blk_id = _lane_block_ids(rb)

    def body(i, carry):
        row0 = pl.multiple_of(i * rb, rb)
        tok = [xf_ref[pl.ds(row0 * ch + s, rb, stride=ch), :].astype(BF16) for s in range(ch)]
        for hf in range(TILES_PER_CHUNK_ROW):
            grp = _transpose_granules(tok[hf * GROUPS_PER_TILE:(hf + 1) * GROUPS_PER_TILE], blk_id)
            for g in range(GROUPS_PER_TILE):
                u_ref[g, pl.ds(row0, rb), hf * LANES:(hf + 1) * LANES] = grp[g]
        return carry

    lax.fori_loop(0, n_chunks // rb, body, 0, unroll=4)
    nc = e_ref.shape[1]
    for g in range(GROUPS_PER_TILE):
        e = jnp.dot(u_ref[g], bm_ref[g], preferred_element_type=F32)
        for b in range(e_ref.shape[0]):
            e_ref[b, :, g, :] = e[b * nc:(b + 1) * nc]


def _s5_scan_kernel(e_ref, a1_ref, a2_ref, a2s_ref, x0_ref):
    a1, a2, a2s = a1_ref[...], a2_ref[...], a2s_ref[...]
    half = a1.shape[-1] // 2

    def body(k, carry):
        s, sw = carry
        x0_ref[0, k] = s
        e = e_ref[0, k]
        e_sw = pltpu.roll(e, half, axis=1)
        return a1 * s + a2 * sw + e, a1 * sw + a2s * s + e_sw

    zero = jnp.zeros(a1.shape, F32)
    lax.fori_loop(0, e_ref.shape[1], body, (zero, zero), unroll=8)


def _s5_out_kernel(u_ref, toep_ref, x0_ref, cmt_ref, d_ref, z_ref, y_ref, zf_ref):
    ch, gc, rb = SSM_CHUNK, SSM_GROUP, RELAYOUT_ROWS
    n_chunks = u_ref.shape[1]
    for g in range(GROUPS_PER_TILE):
        u = u_ref[g]
        x0 = jnp.concatenate([x0_ref[b, :, g, :] for b in range(x0_ref.shape[0])], axis=0).astype(BF16)
        y = (jnp.dot(u, toep_ref[g], preferred_element_type=F32)
             + lax.dot_general(x0, cmt_ref[g], _NT, preferred_element_type=F32)
             + d_ref[g] * u.astype(F32))
        y_ref[g] = jax.nn.gelu(y).astype(BF16)
    blk_id = _lane_block_ids(rb)

    def body(i, carry):
        row0 = pl.multiple_of(i * rb, rb)
        for hf in range(TILES_PER_CHUNK_ROW):
            grp = [y_ref[g, pl.ds(row0, rb), hf * LANES:(hf + 1) * LANES] for g in range(GROUPS_PER_TILE)]
            tok = _transpose_granules(grp, blk_id)
            for j in range(GROUPS_PER_TILE):
                s = hf * GROUPS_PER_TILE + j
                zf_ref[pl.ds(row0 * ch + s, rb, stride=ch), :] = tok[j].astype(F32)
        return carry

    lax.fori_loop(0, n_chunks // rb, body, 0, unroll=4)
    z_ref[...] = zf_ref[...].astype(BF16)


def _s5_core(hn, bsz, seq, ssm, d_skip):
    toep, bmat, cmat, a1, a2, a2s = ssm
    t, d = hn.shape
    g = d // SSM_GROUP
    gc, ch = SSM_GROUP, SSM_CHUNK
    n = gc * ch
    nc = seq // ch
    r = bsz * nc
    p2 = bmat.shape[-1]
    gb = GROUPS_PER_TILE
    assert r % RELAYOUT_ROWS == 0 and g % gb == 0

    u_g, e = pl.pallas_call(
        _s5_state_kernel,
        grid=(g // gb,),
        in_specs=[pl.BlockSpec((t, LANES), lambda i: (0, i)), pl.BlockSpec((gb, n, p2), lambda i: (i, 0, 0))],
        out_specs=[pl.BlockSpec((gb, r, n), lambda i: (i, 0, 0)),
                   pl.BlockSpec((bsz, nc, gb, p2), lambda i: (0, 0, i, 0))],
        out_shape=[jax.ShapeDtypeStruct((g, r, n), BF16), jax.ShapeDtypeStruct((bsz, nc, g, p2), F32)],
        scratch_shapes=[pltpu.VMEM((t, LANES), F32)],
        compiler_params=_compiler_params(("parallel",)),
        name="s5_chunk_state",
    )(hn, bmat)

    gs = min(g, 32)
    tab = pl.BlockSpec((gs, p2), lambda b, i: (i, 0))
    x0 = pl.pallas_call(
        _s5_scan_kernel,
        grid=(bsz, g // gs),
        in_specs=[pl.BlockSpec((1, nc, gs, p2), lambda b, i: (b, 0, i, 0)), tab, tab, tab],
        out_specs=pl.BlockSpec((1, nc, gs, p2), lambda b, i: (b, 0, i, 0)),
        out_shape=jax.ShapeDtypeStruct((bsz, nc, g, p2), F32),
        compiler_params=_compiler_params(("parallel", "parallel")),
        name="s5_chunk_scan",
    )(e, a1, a2, a2s)

    d_tile = jnp.tile(d_skip.astype(F32).reshape(g, 1, gc), (1, 1, ch))
    return pl.pallas_call(
        _s5_out_kernel,
        grid=(g // gb,),
        in_specs=[pl.BlockSpec((gb, r, n), lambda i: (i, 0, 0)), pl.BlockSpec((gb, n, n), lambda i: (i, 0, 0)),
                  pl.BlockSpec((bsz, nc, gb, p2), lambda i: (0, 0, i, 0)),
                  pl.BlockSpec((gb, n, p2), lambda i: (i, 0, 0)), pl.BlockSpec((gb, 1, n), lambda i: (i, 0, 0))],
        out_specs=pl.BlockSpec((t, LANES), lambda i: (0, i)),
        out_shape=jax.ShapeDtypeStruct((t, d), BF16),
        scratch_shapes=[pltpu.VMEM((gb, r, n), BF16), pltpu.VMEM((t, LANES), F32)],
        compiler_params=_compiler_params(("parallel",)),
        name="s5_chunk_out",
    )(u_g, toep, x0, cmat, d_tile)


def _glu_kernel(z_ref, h_ref, w1_ref, w2_ref, b_ref, g_ref, o_ref):
    z = z_ref[...]
    lin = jnp.dot(z, w1_ref[...], preferred_element_type=F32)
    gate = jnp.dot(z, w2_ref[...], preferred_element_type=F32) + b_ref[...]
    o_ref[...] = h_ref[...] + _rms_norm(lin * jax.nn.sigmoid(gate), g_ref[...])


def _glu(z, h, w_out, w_gate, b_gate, g_post, tm=512):
    t, d = h.shape
    tm = _row_tile(t, tm)
    row = pl.BlockSpec((tm, d), lambda i: (i, 0))
    return pl.pallas_call(
        _glu_kernel,
        grid=(t // tm,),
        in_specs=[row, row, _resident((d, d)), _resident((d, d)), _resident((1, d)), _resident((1, d))],
        out_specs=row,
        out_shape=jax.ShapeDtypeStruct((t, d), F32),
        compiler_params=_compiler_params(("parallel",)),
        name="s5_glu",
    )(z, h, w_out, w_gate, b_gate.reshape(1, d), g_post.reshape(1, d))


def _rope_tables(seq):
    half = ROPE_DIM // 2
    inv_freq = ROPE_THETA ** (-jnp.arange(half, dtype=F32) / half)
    ang = jnp.arange(seq, dtype=F32)[:, None] * inv_freq[None, :]
    cos, sin = jnp.cos(ang), jnp.sin(ang)
    ones = jnp.ones((seq, HEAD_DIM - ROPE_DIM), F32)
    zeros = jnp.zeros((seq, HEAD_DIM - half), F32)
    cos_h = jnp.concatenate([cos, cos, ones], axis=1)
    sin_up = jnp.concatenate([-sin, zeros], axis=1)
    sin_dn = jnp.concatenate([jnp.zeros((seq, half), F32), sin, jnp.zeros((seq, HEAD_DIM - ROPE_DIM), F32)], axis=1)
    rep = LANES // HEAD_DIM
    lane_tabs = (jnp.tile(cos_h, (1, rep)), jnp.tile(sin_up, (1, rep)), jnp.tile(sin_dn, (1, rep)))
    return lane_tabs, (cos.T, sin.T)


def _rope_lanes(x, cos, sin_up, sin_dn):
    half = ROPE_DIM // 2
    return (x * cos + pltpu.roll(x, LANES - half, axis=1) * sin_up + pltpu.roll(x, half, axis=1) * sin_dn)


def _kv_kernel(h_ref, g_ref, wk_ref, bk_ref, wvt_ref, bvt_ref, cos_ref, sup_ref, sdn_ref, k_ref, vt_ref):
    hn = _rms_norm(h_ref[...], g_ref[...]).astype(BF16)
    k = jnp.dot(hn, wk_ref[...], preferred_element_type=F32) + bk_ref[...]
    cos, sup, sdn = cos_ref[...], sup_ref[...], sdn_ref[...]
    heads_per_tile = LANES // HEAD_DIM
    for c in range(k.shape[-1] // LANES):
        kr = _rope_lanes(k[:, c * LANES:(c + 1) * LANES], cos, sup, sdn).astype(BF16)
        for e in range(heads_per_tile):
            k_ref[c * heads_per_tile + e] = kr[:, e * HEAD_DIM:(e + 1) * HEAD_DIM]
    vt = lax.dot_general(wvt_ref[...], hn, _NT, preferred_element_type=F32) + bvt_ref[...]
    vt_ref[...] = vt.astype(BF16)


def _shared_kv(h, seq, g, w_kv, b_kv, lane_tabs, tm=512):
    t, d = h.shape
    kw = w_kv.shape[1] // 2
    n_kv = kw // HEAD_DIM
    assert kw % LANES == 0
    tm = _row_tile(seq, tm)
    row = pl.BlockSpec((tm, d), lambda i: (i, 0))
    tab = pl.BlockSpec((tm, LANES), lambda i: (i % (seq // tm), 0))
    wk = w_kv[:, :kw].astype(BF16)
    wvt = w_kv[:, kw:].T.astype(BF16)
    return pl.pallas_call(
        _kv_kernel,
        grid=(t // tm,),
        in_specs=[row, _resident((1, d)), _resident((d, kw)), _resident((1, kw)), _resident((kw, d)),
                  _resident((kw, 1)), tab, tab, tab],
        out_specs=[pl.BlockSpec((n_kv, tm, HEAD_DIM), lambda i: (0, i, 0)), pl.BlockSpec((kw, tm), lambda i: (0, i))],
        out_shape=[jax.ShapeDtypeStruct((n_kv, t, HEAD_DIM), BF16), jax.ShapeDtypeStruct((kw, t), BF16)],
        compiler_params=_compiler_params(("parallel",)),
        name="shared_kv",
    )(h, g.reshape(1, d), wk, b_kv[:kw].reshape(1, kw), wvt, b_kv[kw:].reshape(kw, 1), *lane_tabs)


def _q_kernel(hn_ref, wt_ref, bt_ref, cos_ref, sin_ref, q_ref):
    scale = HEAD_DIM ** -0.5 * LOG2_E
    qt =(lax.dot_general(wt_ref[...], hn_ref[...], _NT, preferred_element_type=F32) + bt_ref[...]) * scale
    q_ref[...] = qt.astype(BF16)
    cos, sin = cos_ref[...], sin_ref[...]
    half = ROPE_DIM // 2
    for head in range(qt.shape[0] // HEAD_DIM):
        r0 = head * HEAD_DIM
        t1, t2 = qt[r0:r0 + half], qt[r0 + half:r0 + ROPE_DIM]
        rot = jnp.concatenate([t1 * cos - t2 * sin, t2 * cos + t1 * sin], axis=0)
        q_ref[r0:r0 + ROPE_DIM, :] = rot.astype(BF16)


def _q_proj(hn, seq, w_q, b_q, freq_tabs, tm=512):
    t, d = hn.shape
    tm = _row_tile(seq, tm)
    half = ROPE_DIM // 2
    tab = pl.BlockSpec((half, tm), lambda i: (0, i % (seq // tm)))
    return pl.pallas_call(
        _q_kernel,
        grid=(t // tm,),
        in_specs=[pl.BlockSpec((tm, d), lambda i: (i, 0)), _resident((d, d)), _resident((d, 1)), tab, tab],
        out_specs=pl.BlockSpec((d, tm), lambda i: (0, i)),
        out_shape=jax.ShapeDtypeStruct((d, t), BF16),
        compiler_params=_compiler_params(("parallel",)),
        name="q_proj",
    )(hn, w_q.T.astype(BF16), b_q.reshape(d, 1), *freq_tabs)


def _attn_kernel(n_kv, qb, sink_ref, q_ref, kc_ref, kp_ref, vc_ref, vp_ref, o_ref):
    blk = ATTN_BLOCK
    width = Q_PER_KV * blk
    step = pl.program_id(1)
    kj = lax.broadcasted_iota(jnp.int32, (2 * blk, width), 0)
    qi = lax.broadcasted_iota(jnp.int32, (2 * blk, width), 1) % blk
    diff = qi + blk - kj
    band = (diff >= 0) & (diff < blk)
    first = band & ((step > 0) | (kj >= blk))
    ones_rows = jnp.ones((8, 2 * blk), BF16)
    for j in range(qb):
        lanes = slice(j * blk, (j + 1) * blk)
        valid = first if j == 0 else band
        for kvh in range(n_kv):
            rows = slice(kvh * HEAD_DIM, (kvh + 1) * HEAD_DIM)
            if j == 0:
                kb = jnp.concatenate([kp_ref[kvh], kc_ref[kvh, :blk]], axis=0)
                vbt = jnp.concatenate([vp_ref[rows, :], vc_ref[rows, :blk]], axis=1)
            else:
                kb = kc_ref[kvh, (j - 1) * blk:(j + 1) * blk]
                vbt = vc_ref[rows, (j - 1) * blk:(j + 1) * blk]
            qcat = jnp.concatenate(
                [q_ref[(kvh * Q_PER_KV + gq) * HEAD_DIM:(kvh * Q_PER_KV + gq + 1) * HEAD_DIM, lanes]
                 for gq in range(Q_PER_KV)], axis=1)
            s = jnp.dot(kb, qcat, preferred_element_type=F32)
            s = jnp.where(valid, s, MASK_VALUE)
            sink = sink_ref[kvh]
            m = jnp.maximum(jnp.max(s, axis=0, keepdims=True), sink)
            p = jnp.exp2(s - m).astype(BF16)
            ov = jnp.dot(jnp.concatenate([vbt, ones_rows], axis=0), p, preferred_element_type=F32)
            den = ov[HEAD_DIM:HEAD_DIM + 1] + jnp.exp2(sink - m)
            o = ov[:HEAD_DIM] * (1.0 / den)
            for gq in range(Q_PER_KV):
                r0 = (kvh * Q_PER_KV + gq) * HEAD_DIM
                o_ref[r0:r0 + HEAD_DIM, lanes] = o[:, gq * blk:(gq + 1) * blk].astype(BF16)


def _attention(qt, k, vt, sinks, bsz, seq, qb=4):
    d, t = qt.shape
    n_kv = k.shape[0]
    blk = ATTN_BLOCK
    qb = min(qb, seq // blk)
    assert seq % (qb * blk) == 0
    tq = qb * blk
    steps = seq // tq
    sink_x = jnp.repeat(sinks.astype(F32) * LOG2_E, blk).reshape(n_kv, 1, Q_PER_KV * blk)
    cur = lambda b, n: b * steps + n
    prev = lambda b, n: b * steps * qb + jnp.maximum(n * qb - 1, 0)
    return pl.pallas_call(
        functools.partial(_attn_kernel, n_kv, qb),
        grid=(bsz, steps),
        in_specs=[_resident((n_kv, 1, Q_PER_KV * blk)),
                  pl.BlockSpec((d, tq), lambda b, n: (0, cur(b, n))),
                  pl.BlockSpec((n_kv, tq, HEAD_DIM), lambda b, n: (0, cur(b, n), 0)),
                  pl.BlockSpec((n_kv, blk, HEAD_DIM), lambda b, n: (0, prev(b, n), 0)),
                  pl.BlockSpec((n_kv * HEAD_DIM, tq), lambda b, n: (0, cur(b, n))),
                  pl.BlockSpec((n_kv * HEAD_DIM, blk), lambda b, n: (0, prev(b, n)))],
        out_specs=pl.BlockSpec((d, tq), lambda b, n: (0, cur(b, n))),
        out_shape=jax.ShapeDtypeStruct((d, t), BF16),
        compiler_params=_compiler_params(("parallel", "parallel")),
        name="swa_sink_attention",
    )(sink_x, qt, k, k, vt, vt)


def _o_kernel(at_ref, h_ref, w_ref, b_ref, g_ref, o_ref):
    mix = lax.dot_general(at_ref[...], w_ref[...], _TN, preferred_element_type=F32) + b_ref[...]
    o_ref[...] = h_ref[...] + _rms_norm(mix, g_ref[...])


def _o_proj(at, h, w_o, b_o, g_post, tm=512):
    t, d = h.shape
    tm = _row_tile(t, tm)
    row = pl.BlockSpec((tm, d), lambda i: (i, 0))
    return pl.pallas_call(
        _o_kernel,
        grid=(t // tm,),
        in_specs=[pl.BlockSpec((d, tm), lambda i: (0, i)), row, _resident((d, d)), _resident((1, d)),
                  _resident((1, d))],
        out_specs=row,
        out_shape=jax.ShapeDtypeStruct((t, d), F32),
        compiler_params=_compiler_params(("parallel",)),
        name="o_proj",
    )(at, h, w_o.astype(BF16), b_o.reshape(1, d), g_post.reshape(1, d))


def kernel(x, norm_g, ffn_w_gate, ffn_w_up, ffn_w_down, ssm_a_re, ssm_a_im, ssm_log_dt, ssm_b_re, ssm_b_im, ssm_c_re, ssm_c_im, ssm_d, glu_w_out, glu_w_gate, glu_b_gate, kv_norm_g, w_kv, b_kv, w_q, b_q, attn_sinks, w_o, b_o):
    bsz, seq, d = x.shape
    depth = norm_g.shape[0]
    n_a = ssm_a_re.shape[0]
    assert seq % ATTN_BLOCK == 0 and seq % SSM_CHUNK == 0 and d % LANES == 0
    h = x.astype(F32).reshape(bsz * seq, d)
    lane_tabs, freq_tabs = _rope_tables(seq)
    ffn_order = [(layer, which) for layer in range(depth) for which in range(2)]
    weights = tuple(w[0, 0].astype(BF16) for w in (ffn_w_gate, ffn_w_up, ffn_w_down))

    def following(layer, which):
        pos = ffn_order.index((layer, which)) + 1
        return (ffn_w_gate, ffn_w_up, ffn_w_down) + ffn_order[pos] if pos < len(ffn_order) else None

    k = vt = None
    for layer in range(depth):
        g = norm_g[layer].astype(F32)
        h, hn, weights = _ffn(h, g[0], g[1], weights, g_next=g[2], next_weights=following(layer, 0))
        if layer < n_a:
            ssm = _s5_params(ssm_a_re[layer], ssm_a_im[layer], ssm_log_dt[layer], ssm_b_re[layer].astype(F32),
                             ssm_b_im[layer].astype(F32), ssm_c_re[layer].astype(F32), ssm_c_im[layer].astype(F32))
            z = _s5_core(hn, bsz, seq, ssm, ssm_d[layer])
            h = _glu(z, h, glu_w_out[layer].astype(BF16), glu_w_gate[layer].astype(BF16), glu_b_gate[layer], g[3])
        else:
            bl = layer - n_a
            qt = _q_proj(hn, seq, w_q[bl], b_q[bl], freq_tabs)
            at = _attention(qt, k, vt, attn_sinks[bl], bsz, seq)
            h = _o_proj(at, h, w_o[bl], b_o[bl], g[3])
        h, _, weights = _ffn(h, g[4], g[5], weights, next_weights=following(layer, 1))
        if layer == n_a - 1:
            k, vt = _shared_kv(h, seq, kv_norm_g.astype(F32), w_kv, b_kv, lane_tabs)
    return h.reshape(bsz, seq, d).astype(x.dtype)
```

```python
import functools
import math

import jax
import jax.numpy as jnp
from jax import lax
from jax.experimental import pallas as pl
from jax.experimental.pallas import tpu as pltpu

F32 = jnp.float32
BF16 = jnp.bfloat16

NORM_EPS = 1e-6
FFN_RESIDUAL_WEIGHT = 0.5
SSM_GROUP = 16
SSM_CHUNK = 16
HEAD_DIM = 64
Q_PER_KV = 8
ATTN_BLOCK = 128
ROPE_DIM = HEAD_DIM // 4
ROPE_THETA = 500000.0
MASK_VALUE = -1e30
LOG2_E = math.log2(math.e)
LANES = 128
VMEM_LIMIT_BYTES = 63 * 1024 * 1024


def _compiler_params(semantics):
    return pltpu.CompilerParams(dimension_semantics=semantics, vmem_limit_bytes=VMEM_LIMIT_BYTES)


def _rms_norm(x, g):
    return x * lax.rsqrt(jnp.mean(x * x, axis=-1, keepdims=True) + NORM_EPS) * g


def _resident(shape):
    return pl.BlockSpec(shape, lambda *_: (0,) * len(shape), pipeline_mode=pl.Buffered(1))


def _row_tile(t, want):
    tm = min(t, want)
    assert t % tm == 0
    return tm


def _ffn_kernel(emit_next, cast_next, h_ref, gpre_ref, gpost_ref, *rest):
    rest = list(rest)
    gnext_ref = rest.pop(0) if emit_next else None
    wg_ref, wu_ref, wd_ref = rest[:3]
    rest = rest[3:]
    next_f32 = [rest.pop(0) for _ in range(3)] if cast_next else []
    o_ref = rest.pop(0)
    on_ref = rest.pop(0) if emit_next else None
    next_bf16 = [rest.pop(0) for _ in range(3)] if cast_next else []
    xn_ref, acc_ref = rest
    j = pl.program_id(1)

    for src, dst in zip(next_f32, next_bf16):
        dst[...] = src[...].astype(BF16)

    def swiglu_chunk():
        xn = xn_ref[...]
        gate = jnp.dot(xn, wg_ref[...], preferred_element_type=F32)
        up = jnp.dot(xn, wu_ref[...], preferred_element_type=F32)
        act = (gate * jax.nn.sigmoid(gate) * up).astype(BF16)
        return jnp.dot(act, wd_ref[...], preferred_element_type=F32)

    @pl.when(j == 0)
    def _():
        xn_ref[...] = _rms_norm(h_ref[...], gpre_ref[...]).astype(BF16)
        acc_ref[...] = swiglu_chunk()

    last = pl.num_programs(1) - 1

    @pl.when((j > 0) & (j < last))
    def _():
        acc_ref[...] += swiglu_chunk()

    @pl.when(j == last)
    def _():
        acc_ref[...] += swiglu_chunk()
        h_new = h_ref[...] + _rms_norm(acc_ref[...], gpost_ref[...])
        o_ref[...] = h_new
        if emit_next:
            on_ref[...] = _rms_norm(h_new, gnext_ref[...]).astype(BF16)


def _ffn(h, g_pre, g_post, weights, g_next=None, next_weights=None, tm=1024, tf=512):
    t, d = h.shape
    w_gate, w_up, w_down = weights
    f = w_gate.shape[-1]
    tm = _row_tile(t, tm)
    tf = _row_tile(f, tf)
    ni, nj = t // tm, f // tf
    emit_next = g_next is not None
    cast_next = next_weights is not None
    row = pl.BlockSpec((tm, d), lambda i, j: (i, 0))
    vec = pl.BlockSpec((1, d), lambda i, j: (0, 0))
    in_specs = [row, vec, vec] + ([vec] if emit_next else []) + [
        pl.BlockSpec((d, tf), lambda i, j: (0, j)),
        pl.BlockSpec((d, tf), lambda i, j: (0, j)),
        pl.BlockSpec((tf, d), lambda i, j: (j, 0)),
    ]
    args = [h, g_pre.reshape(1, d), (FFN_RESIDUAL_WEIGHT * g_post).reshape(1, d)] + (
        [g_next.reshape(1, d)] if emit_next else []) + [w_gate, w_up, w_down]
    out_shape = [jax.ShapeDtypeStruct((t, d), F32)] + ([jax.ShapeDtypeStruct((t, d), BF16)] if emit_next else [])
    out_row = pl.BlockSpec((tm, d), lambda i, j: (i, 0), pipeline_mode=pl.Buffered(1))
    out_specs = [out_row] + ([out_row] if emit_next else [])
    if cast_next:
        nwg, nwu, nwd, layer, which = next_weights
        dr = d // ni
        assert d % ni == 0 and dr % LANES == 0
        in_specs += [pl.BlockSpec((None, None, dr, tf), lambda i, j: (layer, which, i, j)),
                     pl.BlockSpec((None, None, dr, tf), lambda i, j: (layer, which, i, j)),
                     pl.BlockSpec((None, None, tf, dr), lambda i, j: (layer, which, j, i))]
        args += [nwg, nwu, nwd]
        out_specs += [pl.BlockSpec((dr, tf), lambda i, j: (i, j)), pl.BlockSpec((dr, tf), lambda i, j: (i, j)),
                      pl.BlockSpec((tf, dr), lambda i, j: (j, i))]
        out_shape += [jax.ShapeDtypeStruct((d, f), BF16), jax.ShapeDtypeStruct((d, f), BF16),
                      jax.ShapeDtypeStruct((f, d), BF16)]
    outs = pl.pallas_call(
        functools.partial(_ffn_kernel, emit_next, cast_next),
        grid=(ni, nj),
        in_specs=in_specs,
        out_specs=out_specs,
        out_shape=out_shape,
        scratch_shapes=[pltpu.VMEM((tm, d), BF16), pltpu.VMEM((tm, d), F32)],
        compiler_params=_compiler_params(("parallel", "arbitrary")),
        name="ffn",
    )(*args)
    h_new = outs[0]
    normed = outs[1] if emit_next else None
    casted = tuple(outs[-3:]) if cast_next else None
    return h_new, normed, casted


_NT = (((1,), (1,)), ((), ()))
_TN = (((0,), (0,)), ((), ()))


def _s5_param_kernel(z0r_ref, z0i_ref, z1r_ref, z1i_ref, zvr_ref, zvi_ref, cr_ref, ci_ref, bbr_ref, bbi_ref,
                     kt_ref, bm_ref, cmt_ref):
    rows, p = z0r_ref.shape[1], z0r_ref.shape[2]

    def rep(a):
        return jnp.broadcast_to(a[:, None, :], (rows, SSM_GROUP, p)).reshape(rows * SSM_GROUP, p)

    def til(a):
        return jnp.broadcast_to(a[None, :, :], (rows, SSM_GROUP, p)).reshape(rows * SSM_GROUP, p)

    for gi in range(z0r_ref.shape[0]):
        cr, ci = til(cr_ref[gi]), til(ci_ref[gi])
        z0r, z0i = rep(z0r_ref[gi]), rep(z0i_ref[gi])
        zc_r = z0r * cr - z0i * ci
        zc_i = z0r * ci + z0i * cr
        kt_ref[gi] = (
            lax.dot_general(bbr_ref[gi], zc_r, _NT, precision=lax.Precision.HIGHEST, preferred_element_type=F32)
            - lax.dot_general(bbi_ref[gi], zc_i, _NT, precision=lax.Precision.HIGHEST, preferred_element_type=F32))
        z1r, z1i = rep(z1r_ref[gi]), rep(z1i_ref[gi])
        cmt_ref[gi] = jnp.concatenate([z1r * cr - z1i * ci, -(z1r * ci + z1i * cr)], axis=1).astype(BF16)
        zvr, zvi = rep(zvr_ref[gi]), rep(zvi_ref[gi])
        btr, bti = til(bbr_ref[gi]), til(bbi_ref[gi])
        bm_ref[gi] = jnp.concatenate([zvr * btr - zvi * bti, zvr * bti + zvi * btr], axis=1).astype(BF16)


def _s5_params(a_re, a_im, log_dt, b_re, b_im, c_re, c_im):
    g, p = a_re.shape
    gc, ch = SSM_GROUP, SSM_CHUNK
    n = ch * gc
    dt = jnp.exp(log_dt.astype(F32))[:, None]
    lam_re = a_re.astype(F32) * dt
    lam_im = a_im.astype(F32) * dt
    lags = jnp.arange(ch + 1, dtype=F32)[None, :, None]
    mag = jnp.exp(lags * lam_re[:, None, :])
    zr = mag * jnp.cos(lags * lam_im[:, None, :])
    zi = mag * jnp.sin(lags * lam_im[:, None, :])
    lb_re, lb_im = zr[:, 1], zi[:, 1]
    den = a_re * a_re + a_im * a_im
    num_re = lb_re - 1.0
    f_re = (num_re * a_re + lb_im * a_im) / den
    f_im = (lb_im * a_re - num_re * a_im) / den
    bbt_re = f_re[:, None, :] * jnp.swapaxes(b_re, 1, 2) - f_im[:, None, :] * jnp.swapaxes(b_im, 1, 2)
    bbt_im = f_re[:, None, :] * jnp.swapaxes(b_im, 1, 2) + f_im[:, None, :] * jnp.swapaxes(b_re, 1, 2)

    gp = min(g, 8)
    assert g % gp == 0
    blk = lambda *s: pl.BlockSpec((gp,) + s, lambda i: (i, 0, 0))
    kt, bmat, cmat_t = pl.pallas_call(
        _s5_param_kernel,
        grid=(g // gp,),
        in_specs=[blk(ch, p)] * 6 + [blk(gc, p)] * 4,
        out_specs=[blk(gc, n), blk(n, 2 * p), blk(n, 2 * p)],
        out_shape=[jax.ShapeDtypeStruct((g, gc, n), F32), jax.ShapeDtypeStruct((g, n, 2 * p), BF16),
                   jax.ShapeDtypeStruct((g, n, 2 * p), BF16)],
        compiler_params=_compiler_params(("parallel",)),
        name="s5_params",
    )(zr[:, :ch], zi[:, :ch], zr[:, 1:], zi[:, 1:], zr[:, ch - 1::-1][:, :ch], zi[:, ch - 1::-1][:, :ch],
      c_re, c_im, bbt_re, bbt_im)

    kt_pad = jnp.pad(kt, ((0, 0), (0, 0), (n, 0)))
    toep = jnp.stack([kt_pad[:, :, n - s * gc:2 * n - s * gc] for s in range(ch)], axis=1)
    toep = toep.reshape(g, n, n).astype(BF16)
    zl_r, zl_i = zr[:, ch], zi[:, ch]
    a1 = jnp.concatenate([zl_r, zl_r], axis=-1)
    a2 = jnp.concatenate([-zl_i, zl_i], axis=-1)
    a2s = jnp.concatenate([zl_i, -zl_i], axis=-1)
    return toep, bmat, cmat_t, a1, a2, a2s


GROUPS_PER_TILE = LANES // SSM_GROUP
TILES_PER_CHUNK_ROW = SSM_CHUNK * SSM_GROUP // LANES
RELAYOUT_ROWS = 16


def _lane_block_ids(rows):
    return lax.broadcasted_iota(jnp.int32, (rows, LANES), 1) // SSM_GROUP


def _transpose_granules(v, blk_id):
    v = list(v)
    d = GROUPS_PER_TILE // 2
    while d >= 1:
        low = (blk_id & d) == 0
        nxt = list(v)
        for j in range(GROUPS_PER_TILE):
            if j & d:
                continue
            a, b = v[j], v[j + d]
            nxt[j] = jnp.where(low, a, pltpu.roll(b, d * SSM_GROUP, axis=1))
            nxt[j + d] = jnp.where(low, pltpu.roll(a, LANES - d * SSM_GROUP, axis=1), b)
        v = nxt
        d //= 2
    return v


def _s5_state_kernel(x_ref, bm_ref, u_ref, e_ref, xf_ref):
    ch, gc, rb = SSM_CHUNK, SSM_GROUP, RELAYOUT_ROWS
    n_chunks = u_ref.shape[1]
    xf_ref[...] = x_ref[...].astype(F32)
    blk_id = _lane_block_ids(rb)

    def body(i, carry):
        row0 = pl.multiple_of(i * rb, rb)
        tok = [xf_ref[pl.ds(row0 * ch + s, rb, stride=ch), :].astype(BF16) for s in range(ch)]
        for hf in range(TILES_PER_CHUNK_ROW):
            grp = _transpose_granules(tok[hf * GROUPS_PER_TILE:(hf + 1) * GROUPS_PER_TILE], blk_id)
            for g in range(GROUPS_PER_TILE):
                u_ref[g, pl.ds(row0, rb), hf * LANES:(hf + 1) * LANES] = grp[g]
        return carry

    lax.fori_loop(0, n_chunks // rb, body, 0, unroll=4)
    nc = e_ref.shape[1]
    for g in range(GROUPS_PER_TILE):
        e = jnp.dot(u_ref[g], bm_ref[g], preferred_element_type=F32)
        for b in range(e_ref.shape[0]):
            e_ref[b, :, g, :] = e[b * nc:(b + 1) * nc]


def _s5_scan_kernel(e_ref, a1_ref, a2_ref, a2s_ref, x0_ref):
    a1, a2, a2s = a1_ref[...], a2_ref[...], a2s_ref[...]
    half = a1.shape[-1] // 2

    def body(k, carry):
        s, sw = carry
        x0_ref[0, k] = s
        e = e_ref[0, k]
        e_sw = pltpu.roll(e, half, axis=1)
        return a1 * s + a2 * sw + e, a1 * sw + a2s * s + e_sw

    zero = jnp.zeros(a1.shape, F32)
    lax.fori_loop(0, e_ref.shape[1], body, (zero, zero), unroll=8)


def _s5_out_kernel(u_ref, toep_ref, x0_ref, cmt_ref, d_ref, z_ref, y_ref, zf_ref):
    ch, gc, rb = SSM_CHUNK, SSM_GROUP, RELAYOUT_ROWS
    n_chunks = u_ref.shape[1]
    for g in range(GROUPS_PER_TILE):
        u = u_ref[g]
        x0 = jnp.concatenate([x0_ref[b, :, g, :] for b in range(x0_ref.shape[0])], axis=0).astype(BF16)
        y = (jnp.dot(u, toep_ref[g], preferred_element_type=F32)
             + lax.dot_general(x0, cmt_ref[g], _NT, preferred_element_type=F32)
             + d_ref[g] * u.astype(F32))
        y_ref[g] = jax.nn.gelu(y).astype(BF16)
    blk_id = _lane_block_ids(rb)

    def body(i, carry):
        row0 = pl.multiple_of(i * rb, rb)
        for hf in range(TILES_PER_CHUNK_ROW):
            grp = [y_ref[g, pl.ds(row0, rb), hf * LANES:(hf + 1) * LANES] for g in range(GROUPS_PER_TILE)]
            tok = _transpose_granules(grp, blk_id)
            for j in range(GROUPS_PER_TILE):
                s = hf * GROUPS_PER_TILE + j
                zf_ref[pl.ds(row0 * ch + s, rb, stride=ch), :] = tok[j].astype(F32)
        return carry

    lax.fori_loop(0, n_chunks // rb, body, 0, unroll=4)
    z_ref[...] = zf_ref[...].astype(BF16)


def _s5_core(hn, bsz, seq, ssm, d_skip):
    toep, bmat, cmat, a1, a2, a2s = ssm
    t, d = hn.shape
    g = d // SSM_GROUP
    gc, ch = SSM_GROUP, SSM_CHUNK
    n = gc * ch
    nc = seq // ch
    r = bsz * nc
    p2 = bmat.shape[-1]
    gb = GROUPS_PER_TILE
    assert r % RELAYOUT_ROWS == 0 and g % gb == 0

    u_g, e = pl.pallas_call(
        _s5_state_kernel,
        grid=(g // gb,),
        in_specs=[pl.BlockSpec((t, LANES), lambda i: (0, i)), pl.BlockSpec((gb, n, p2), lambda i: (i, 0, 0))],
        out_specs=[pl.BlockSpec((gb, r, n), lambda i: (i, 0, 0)),
                   pl.BlockSpec((bsz, nc, gb, p2), lambda i: (0, 0, i, 0))],
        out_shape=[jax.ShapeDtypeStruct((g, r, n), BF16), jax.ShapeDtypeStruct((bsz, nc, g, p2), F32)],
        scratch_shapes=[pltpu.VMEM((t, LANES), F32)],
        compiler_params=_compiler_params(("parallel",)),
        name="s5_chunk_state",
    )(hn, bmat)

    gs = min(g, 32)
    tab = pl.BlockSpec((gs, p2), lambda b, i: (i, 0))
    x0 = pl.pallas_call(
        _s5_scan_kernel,
        grid=(bsz, g // gs),
        in_specs=[pl.BlockSpec((1, nc, gs, p2), lambda b, i: (b, 0, i, 0)), tab, tab, tab],
        out_specs=pl.BlockSpec((1, nc, gs, p2), lambda b, i: (b, 0, i, 0)),
        out_shape=jax.ShapeDtypeStruct((bsz, nc, g, p2), F32),
        compiler_params=_compiler_params(("parallel", "parallel")),
        name="s5_chunk_scan",
    )(e, a1, a2, a2s)

    d_tile = jnp.tile(d_skip.astype(F32).reshape(g, 1, gc), (1, 1, ch))
    return pl.pallas_call(
        _s5_out_kernel,
        grid=(g // gb,),
        in_specs=[pl.BlockSpec((gb, r, n), lambda i: (i, 0, 0)), pl.BlockSpec((gb, n, n), lambda i: (i, 0, 0)),
                  pl.BlockSpec((bsz, nc, gb, p2), lambda i: (0, 0, i, 0)),
                  pl.BlockSpec((gb, n, p2), lambda i: (i, 0, 0)), pl.BlockSpec((gb, 1, n), lambda i: (i, 0, 0))],
        out_specs=pl.BlockSpec((t, LANES), lambda i: (0, i)),
        out_shape=jax.ShapeDtypeStruct((t, d), BF16),
        scratch_shapes=[pltpu.VMEM((gb, r, n), BF16), pltpu.VMEM((t, LANES), F32)],
        compiler_params=_compiler_params(("parallel",)),
        name="s5_chunk_out",
    )(u_g, toep, x0, cmat, d_tile)


def _glu_kernel(z_ref, h_ref, w1_ref, w2_ref, b_ref, g_ref, o_ref):
    z = z_ref[...]
    lin = jnp.dot(z, w1_ref[...], preferred_element_type=F32)
    gate = jnp.dot(z, w2_ref[...], preferred_element_type=F32) + b_ref[...]
    o_ref[...] = h_ref[...] + _rms_norm(lin * jax.nn.sigmoid(gate), g_ref[...])


def _glu(z, h, w_out, w_gate, b_gate, g_post, tm=512):
    t, d = h.shape
    tm = _row_tile(t, tm)
    row = pl.BlockSpec((tm, d), lambda i: (i, 0))
    return pl.pallas_call(
        _glu_kernel,
        grid=(t // tm,),
        in_specs=[row, row, _resident((d, d)), _resident((d, d)), _resident((1, d)), _resident((1, d))],
        out_specs=row,
        out_shape=jax.ShapeDtypeStruct((t, d), F32),
        compiler_params=_compiler_params(("parallel",)),
        name="s5_glu",
    )(z, h, w_out, w_gate, b_gate.reshape(1, d), g_post.reshape(1, d))


def _rope_tables(seq):
    half = ROPE_DIM // 2
    inv_freq = ROPE_THETA ** (-jnp.arange(half, dtype=F32) / half)
    ang = jnp.arange(seq, dtype=F32)[:, None] * inv_freq[None, :]
    cos, sin = jnp.cos(ang), jnp.sin(ang)
    ones = jnp.ones((seq, HEAD_DIM - ROPE_DIM), F32)
    zeros = jnp.zeros((seq, HEAD_DIM - half), F32)
    cos_h = jnp.concatenate([cos, cos, ones], axis=1)
    sin_up = jnp.concatenate([-sin, zeros], axis=1)
    sin_dn = jnp.concatenate([jnp.zeros((seq, half), F32), sin, jnp.zeros((seq, HEAD_DIM - ROPE_DIM), F32)], axis=1)
    rep = LANES // HEAD_DIM
    lane_tabs = (jnp.tile(cos_h, (1, rep)), jnp.tile(sin_up, (1, rep)), jnp.tile(sin_dn, (1, rep)))
    return lane_tabs, (cos.T, sin.T)


def _rope_lanes(x, cos, sin_up, sin_dn):
    half = ROPE_DIM // 2
    return (x * cos + pltpu.roll(x, LANES - half, axis=1) * sin_up + pltpu.roll(x, half, axis=1) * sin_dn)


def _kv_kernel(hn_ref, wk_ref, bk_ref, wvt_ref, bvt_ref, cos_ref, sup_ref, sdn_ref, k_ref, vt_ref):
    hn = hn_ref[...]
    k = jnp.dot(hn, wk_ref[...], preferred_element_type=F32) + bk_ref[...]
    cos, sup, sdn = cos_ref[...], sup_ref[...], sdn_ref[...]
    heads_per_tile = LANES // HEAD_DIM
    for c in range(k.shape[-1] // LANES):
        kr = _rope_lanes(k[:, c * LANES:(c + 1) * LANES], cos, sup, sdn).astype(BF16)
        for e in range(heads_per_tile):
            k_ref[c * heads_per_tile + e] = kr[:, e * HEAD_DIM:(e + 1) * HEAD_DIM]
    vt = lax.dot_general(wvt_ref[...], hn, _NT, preferred_element_type=F32) + bvt_ref[...]
    vt_ref[...] = vt.astype(BF16)


def _shared_kv(hn, seq, w_kv, b_kv, lane_tabs, tm=512):
    t, d = hn.shape
    kw = w_kv.shape[1] // 2
    n_kv = kw // HEAD_DIM
    assert kw % LANES == 0
    tm = _row_tile(seq, tm)
    row = pl.BlockSpec((tm, d), lambda i: (i, 0))
    tab = pl.BlockSpec((tm, LANES), lambda i: (i % (seq // tm), 0))
    wk = w_kv[:, :kw].astype(BF16)
    wvt = w_kv[:, kw:].T.astype(BF16)
    return pl.pallas_call(
        _kv_kernel,
        grid=(t // tm,),
        in_specs=[row, _resident((d, kw)), _resident((1, kw)), _resident((kw, d)), _resident((kw, 1)),
                  tab, tab, tab],
        out_specs=[pl.BlockSpec((n_kv, tm, HEAD_DIM), lambda i: (0, i, 0)), pl.BlockSpec((kw, tm), lambda i: (0, i))],
        out_shape=[jax.ShapeDtypeStruct((n_kv, t, HEAD_DIM), BF16), jax.ShapeDtypeStruct((kw, t), BF16)],
        compiler_params=_compiler_params(("parallel",)),
        name="shared_kv",
    )(hn, wk, b_kv[:kw].reshape(1, kw), wvt, b_kv[kw:].reshape(kw, 1), *lane_tabs)


def _q_kernel(hn_ref, wt_ref, bt_ref, cos_ref, sin_ref, q_ref):
    scale = HEAD_DIM ** -0.5 * LOG2_E
    qt =(lax.dot_general(wt_ref[...], hn_ref[...], _NT, preferred_element_type=F32) + bt_ref[...]) * scale
    q_ref[...] = qt.astype(BF16)
    cos, sin = cos_ref[...], sin_ref[...]
    half = ROPE_DIM // 2
    for head in range(qt.shape[0] // HEAD_DIM):
        r0 = head * HEAD_DIM
        t1, t2 = qt[r0:r0 + half], qt[r0 + half:r0 + ROPE_DIM]
        rot = jnp.concatenate([t1 * cos - t2 * sin, t2 * cos + t1 * sin], axis=0)
        q_ref[r0:r0 + ROPE_DIM, :] = rot.astype(BF16)


def _q_proj(hn, seq, w_q, b_q, freq_tabs, tm=512):
    t, d = hn.shape
    tm = _row_tile(seq, tm)
    half = ROPE_DIM // 2
    tab = pl.BlockSpec((half, tm), lambda i: (0, i % (seq // tm)))
    return pl.pallas_call(
        _q_kernel,
        grid=(t // tm,),
        in_specs=[pl.BlockSpec((tm, d), lambda i: (i, 0)), _resident((d, d)), _resident((d, 1)), tab, tab],
        out_specs=pl.BlockSpec((d, tm), lambda i: (0, i)),
        out_shape=jax.ShapeDtypeStruct((d, t), BF16),
        compiler_params=_compiler_params(("parallel",)),
        name="q_proj",
    )(hn, w_q.T.astype(BF16), b_q.reshape(d, 1), *freq_tabs)


def _attn_kernel(n_kv, qb, sink_ref, q_ref, kc_ref, kp_ref, vc_ref, vp_ref, o_ref):
    blk = ATTN_BLOCK
    width = Q_PER_KV * blk
    step = pl.program_id(1)
    r = lax.broadcasted_iota(jnp.int32, (blk, width), 0)
    qi = lax.broadcasted_iota(jnp.int32, (blk, width), 1) % blk
    from_prev = r > qi
    has_prev = (step > 0) | (r < 0)
    zero = jnp.zeros((blk, width), BF16)
    ones_rows = jnp.ones((8, 2 * blk), BF16)
    for j in range(qb):
        lanes = slice(j * blk, (j + 1) * blk)
        for kvh in range(n_kv):
            rows = slice(kvh * HEAD_DIM, (kvh + 1) * HEAD_DIM)
            if j == 0:
                kb = jnp.concatenate([kp_ref[kvh], kc_ref[kvh, :blk]], axis=0)
                vbt = jnp.concatenate([vp_ref[rows, :], vc_ref[rows, :blk]], axis=1)
            else:
                kb = kc_ref[kvh, (j - 1) * blk:(j + 1) * blk]
                vbt = vc_ref[rows, (j - 1) * blk:(j + 1) * blk]
            qcat = jnp.concatenate(
                [q_ref[(kvh * Q_PER_KV + gq) * HEAD_DIM:(kvh * Q_PER_KV + gq + 1) * HEAD_DIM, lanes]
                 for gq in range(Q_PER_KV)], axis=1)
            s2 = jnp.dot(kb, qcat, preferred_element_type=F32)
            s_prev = s2[:blk]
            if j == 0:
                s_prev = jnp.where(has_prev, s_prev, MASK_VALUE)
            s = jnp.where(from_prev, s_prev, s2[blk:])
            sink = sink_ref[kvh]
            m = jnp.maximum(jnp.max(s, axis=0, keepdims=True), sink)
            p = jnp.exp2(s - m).astype(BF16)
            p2 = jnp.concatenate([jnp.where(from_prev, p, zero), jnp.where(from_prev, zero, p)], axis=0)
            ov = jnp.dot(jnp.concatenate([vbt, ones_rows], axis=0), p2, preferred_element_type=F32)
            den = ov[HEAD_DIM:HEAD_DIM + 1] + jnp.exp2(sink - m)
            o = ov[:HEAD_DIM] * (1.0 / den)
            for gq in range(Q_PER_KV):
                r0 = (kvh * Q_PER_KV + gq) * HEAD_DIM
                o_ref[r0:r0 + HEAD_DIM, lanes] = o[:, gq * blk:(gq + 1) * blk].astype(BF16)


def _attention(qt, k, vt, sinks, bsz, seq, qb=4):
    d, t = qt.shape
    n_kv = k.shape[0]
    blk = ATTN_BLOCK
    qb = min(qb, seq // blk)
    assert seq % (qb * blk) == 0
    tq = qb * blk
    steps = seq // tq
    sink_x = jnp.repeat(sinks.astype(F32) * LOG2_E, blk).reshape(n_kv, 1, Q_PER_KV * blk)
    cur = lambda b, n: b * steps + n
    prev = lambda b, n: b * steps * qb + jnp.maximum(n * qb - 1, 0)
    return pl.pallas_call(
        functools.partial(_attn_kernel, n_kv, qb),
        grid=(bsz, steps),
        in_specs=[_resident((n_kv, 1, Q_PER_KV * blk)),
                  pl.BlockSpec((d, tq), lambda b, n: (0, cur(b, n))),
                  pl.BlockSpec((n_kv, tq, HEAD_DIM), lambda b, n: (0, cur(b, n), 0)),
                  pl.BlockSpec((n_kv, blk, HEAD_DIM), lambda b, n: (0, prev(b, n), 0)),
                  pl.BlockSpec((n_kv * HEAD_DIM, tq), lambda b, n: (0, cur(b, n))),
                  pl.BlockSpec((n_kv * HEAD_DIM, blk), lambda b, n: (0, prev(b, n)))],
        out_specs=pl.BlockSpec((d, tq), lambda b, n: (0, cur(b, n))),
        out_shape=jax.ShapeDtypeStruct((d, t), BF16),
        compiler_params=_compiler_params(("parallel", "parallel")),
        name="swa_sink_attention",
    )(sink_x, qt, k, k, vt, vt)


def _o_kernel(at_ref, h_ref, w_ref, b_ref, g_ref, o_ref):
    mix = lax.dot_general(at_ref[...], w_ref[...], _TN, preferred_element_type=F32) + b_ref[...]
    o_ref[...] = h_ref[...] + _rms_norm(mix, g_ref[...])


def _o_proj(at, h, w_o, b_o, g_post, tm=512):
    t, d = h.shape
    tm = _row_tile(t, tm)
    row = pl.BlockSpec((tm, d), lambda i: (i, 0))
    return pl.pallas_call(
        _o_kernel,
        grid=(t // tm,),
        in_specs=[pl.BlockSpec((d, tm), lambda i: (0, i)), row, _resident((d, d)), _resident((1, d)),
                  _resident((1, d))],
        out_specs=row,
        out_shape=jax.ShapeDtypeStruct((t, d), F32),
        compiler_params=_compiler_params(("parallel",)),
        name="o_proj",
    )(at, h, w_o.astype(BF16), b_o.reshape(1, d), g_post.reshape(1, d))


def kernel(x, norm_g, ffn_w_gate, ffn_w_up, ffn_w_down, ssm_a_re, ssm_a_im, ssm_log_dt, ssm_b_re, ssm_b_im, ssm_c_re, ssm_c_im, ssm_d, glu_w_out, glu_w_gate, glu_b_gate, kv_norm_g, w_kv, b_kv, w_q, b_q, attn_sinks, w_o, b_o):
    bsz, seq, d = x.shape
    depth = norm_g.shape[0]
    n_a = ssm_a_re.shape[0]
    assert seq % ATTN_BLOCK == 0 and seq % SSM_CHUNK == 0 and d % LANES == 0
    h = x.astype(F32).reshape(bsz * seq, d)
    lane_tabs, freq_tabs = _rope_tables(seq)
    ffn_order = [(layer, which) for layer in range(depth) for which in range(2)]
    weights = tuple(w[0, 0].astype(BF16) for w in (ffn_w_gate, ffn_w_up, ffn_w_down))

    def following(layer, which):
        pos = ffn_order.index((layer, which)) + 1
        return (ffn_w_gate, ffn_w_up, ffn_w_down) + ffn_order[pos] if pos < len(ffn_order) else None

    k = vt = None
    for layer in range(depth):
        g = norm_g[layer].astype(F32)
        h, hn, weights = _ffn(h, g[0], g[1], weights, g_next=g[2], next_weights=following(layer, 0))
        if layer < n_a:
            ssm = _s5_params(ssm_a_re[layer], ssm_a_im[layer], ssm_log_dt[layer], ssm_b_re[layer].astype(F32),
                             ssm_b_im[layer].astype(F32), ssm_c_re[layer].astype(F32), ssm_c_im[layer].astype(F32))
            z = _s5_core(hn, bsz, seq, ssm, ssm_d[layer])
            h = _glu(z, h, glu_w_out[layer].astype(BF16), glu_w_gate[layer].astype(BF16), glu_b_gate[layer], g[3])
        else:
            bl = layer - n_a
            qt = _q_proj(hn, seq, w_q[bl], b_q[bl], freq_tabs)
            at = _attention(qt, k, vt, attn_sinks[bl], bsz, seq)
            h = _o_proj(at, h, w_o[bl], b_o[bl], g[3])
        g_kv = kv_norm_g.astype(F32) if layer == n_a - 1 else None
        h, hn_kv, weights = _ffn(h, g[4], g[5], weights, g_next=g_kv, next_weights=following(layer, 1))
        if layer == n_a - 1:
            k, vt = _shared_kv(hn_kv, seq, w_kv, b_kv, lane_tabs)
    return h.reshape(bsz, seq, d).astype(x.dtype)
```

```python
import functools
import math

import jax
import jax.numpy as jnp
from jax import lax
from jax.experimental import pallas as pl
from jax.experimental.pallas import tpu as pltpu

F32 = jnp.float32
BF16 = jnp.bfloat16

NORM_EPS = 1e-6
FFN_RESIDUAL_WEIGHT = 0.5
SSM_GROUP = 16
SSM_CHUNK = 16
HEAD_DIM = 64
Q_PER_KV = 8
ATTN_BLOCK = 128
ROPE_DIM = HEAD_DIM // 4
ROPE_THETA = 500000.0
MASK_VALUE = -1e30
LOG2_E = math.log2(math.e)
LANES = 128
VMEM_LIMIT_BYTES = 63 * 1024 * 1024


def _compiler_params(semantics):
    return pltpu.CompilerParams(dimension_semantics=semantics, vmem_limit_bytes=VMEM_LIMIT_BYTES)


def _rms_norm(x, g):
    return x * lax.rsqrt(jnp.mean(x * x, axis=-1, keepdims=True) + NORM_EPS) * g


def _resident(shape):
    return pl.BlockSpec(shape, lambda *_: (0,) * len(shape), pipeline_mode=pl.Buffered(1))


def _row_tile(t, want):
    tm = min(t, want)
    assert t % tm == 0
    return tm


def _ffn_kernel(emit_next, cast_next, h_ref, gpre_ref, gpost_ref, *rest):
    rest = list(rest)
    gnext_ref = rest.pop(0) if emit_next else None
    wg_ref, wu_ref, wd_ref = rest[:3]
    rest = rest[3:]
    next_f32 = [rest.pop(0) for _ in range(3)] if cast_next else []
    o_ref = rest.pop(0)
    on_ref = rest.pop(0) if emit_next else None
    next_bf16 = [rest.pop(0) for _ in range(3)] if cast_next else []
    xn_ref, acc_ref = rest
    j = pl.program_id(1)

    for src, dst in zip(next_f32, next_bf16):
        dst[...] = src[...].astype(BF16)

    def swiglu_chunk():
        xn = xn_ref[...]
        gate = jnp.dot(xn, wg_ref[...], preferred_element_type=F32)
        up = jnp.dot(xn, wu_ref[...], preferred_element_type=F32)
        act = (gate * jax.nn.sigmoid(gate) * up).astype(BF16)
        return jnp.dot(act, wd_ref[...], preferred_element_type=F32)

    @pl.when(j == 0)
    def _():
        xn_ref[...] = _rms_norm(h_ref[...], gpre_ref[...]).astype(BF16)
        acc_ref[...] = swiglu_chunk()

    last = pl.num_programs(1) - 1

    @pl.when((j > 0) & (j < last))
    def _():
        acc_ref[...] += swiglu_chunk()

    @pl.when(j == last)
    def _():
        acc_ref[...] += swiglu_chunk()
        h_new = h_ref[...] + _rms_norm(acc_ref[...], gpost_ref[...])
        o_ref[...] = h_new
        if emit_next:
            on_ref[...] = _rms_norm(h_new, gnext_ref[...]).astype(BF16)


def _ffn(h, g_pre, g_post, weights, g_next=None, next_weights=None, tm=1024, tf=512):
    t, d = h.shape
    w_gate, w_up, w_down = weights
    f = w_gate.shape[-1]
    tm = _row_tile(t, tm)
    tf = _row_tile(f, tf)
    ni, nj = t // tm, f // tf
    emit_next = g_next is not None
    cast_next = next_weights is not None
    row = pl.BlockSpec((tm, d), lambda i, j: (i, 0))
    vec = pl.BlockSpec((1, d), lambda i, j: (0, 0))
    in_specs = [row, vec, vec] + ([vec] if emit_next else []) + [
        pl.BlockSpec((d, tf), lambda i, j: (0, j)),
        pl.BlockSpec((d, tf), lambda i, j: (0, j)),
        pl.BlockSpec((tf, d), lambda i, j: (j, 0)),
    ]
    args = [h, g_pre.reshape(1, d), (FFN_RESIDUAL_WEIGHT * g_post).reshape(1, d)] + (
        [g_next.reshape(1, d)] if emit_next else []) + [w_gate, w_up, w_down]
    out_shape = [jax.ShapeDtypeStruct((t, d), F32)] + ([jax.ShapeDtypeStruct((t, d), BF16)] if emit_next else [])
    out_row = pl.BlockSpec((tm, d), lambda i, j: (i, 0), pipeline_mode=pl.Buffered(1))
    out_specs = [out_row] + ([out_row] if emit_next else [])
    if cast_next:
        nwg, nwu, nwd, layer, which = next_weights
        dr = d // ni
        assert d % ni == 0 and dr % LANES == 0
        in_specs += [pl.BlockSpec((None, None, dr, tf), lambda i, j: (layer, which, i, j)),
                     pl.BlockSpec((None, None, dr, tf), lambda i, j: (layer, which, i, j)),
                     pl.BlockSpec((None, None, tf, dr), lambda i, j: (layer, which, j, i))]
        args += [nwg, nwu, nwd]
        out_specs += [pl.BlockSpec((dr, tf), lambda i, j: (i, j)), pl.BlockSpec((dr, tf), lambda i, j: (i, j)),
                      pl.BlockSpec((tf, dr), lambda i, j: (j, i))]
        out_shape += [jax.ShapeDtypeStruct((d, f), BF16), jax.ShapeDtypeStruct((d, f), BF16),
                      jax.ShapeDtypeStruct((f, d), BF16)]
    outs = pl.pallas_call(
        functools.partial(_ffn_kernel, emit_next, cast_next),
        grid=(ni, nj),
        in_specs=in_specs,
        out_specs=out_specs,
        out_shape=out_shape,
        scratch_shapes=[pltpu.VMEM((tm, d), BF16), pltpu.VMEM((tm, d), F32)],
        compiler_params=_compiler_params(("parallel", "arbitrary")),
        name="ffn",
    )(*args)
    h_new = outs[0]
    normed = outs[1] if emit_next else None
    casted = tuple(outs[-3:]) if cast_next else None
    return h_new, normed, casted


_NT = (((1,), (1,)), ((), ()))
_TN = (((0,), (0,)), ((), ()))


def _s5_param_kernel(z0r_ref, z0i_ref, z1r_ref, z1i_ref, zvr_ref, zvi_ref, cr_ref, ci_ref, bbr_ref, bbi_ref,
                     kt_ref, bm_ref, cmt_ref):
    rows, p = z0r_ref.shape[1], z0r_ref.shape[2]

    def rep(a):
        return jnp.broadcast_to(a[:, None, :], (rows, SSM_GROUP, p)).reshape(rows * SSM_GROUP, p)

    def til(a):
        return jnp.broadcast_to(a[None, :, :], (rows, SSM_GROUP, p)).reshape(rows * SSM_GROUP, p)

    for gi in range(z0r_ref.shape[0]):
        cr, ci = til(cr_ref[gi]), til(ci_ref[gi])
        z0r, z0i = rep(z0r_ref[gi]), rep(z0i_ref[gi])
        zc_r = z0r * cr - z0i * ci
        zc_i = z0r * ci + z0i * cr
        kt_ref[gi] = (
            lax.dot_general(bbr_ref[gi], zc_r, _NT, precision=lax.Precision.HIGHEST, preferred_element_type=F32)
            - lax.dot_general(bbi_ref[gi], zc_i, _NT, precision=lax.Precision.HIGHEST, preferred_element_type=F32))
        z1r, z1i = rep(z1r_ref[gi]), rep(z1i_ref[gi])
        cmt_ref[gi] = jnp.concatenate([z1r * cr - z1i * ci, -(z1r * ci + z1i * cr)], axis=1).astype(BF16)
        zvr, zvi = rep(zvr_ref[gi]), rep(zvi_ref[gi])
        btr, bti = til(bbr_ref[gi]), til(bbi_ref[gi])
        bm_ref[gi] = jnp.concatenate([zvr * btr - zvi * bti, zvr * bti + zvi * btr], axis=1).astype(BF16)


def _s5_params(a_re, a_im, log_dt, b_re, b_im, c_re, c_im):
    g, p = a_re.shape
    gc, ch = SSM_GROUP, SSM_CHUNK
    n = ch * gc
    dt = jnp.exp(log_dt.astype(F32))[:, None]
    lam_re = a_re.astype(F32) * dt
    lam_im = a_im.astype(F32) * dt
    lags = jnp.arange(ch + 1, dtype=F32)[None, :, None]
    mag = jnp.exp(lags * lam_re[:, None, :])
    zr = mag * jnp.cos(lags * lam_im[:, None, :])
    zi = mag * jnp.sin(lags * lam_im[:, None, :])
    lb_re, lb_im = zr[:, 1], zi[:, 1]
    den = a_re * a_re + a_im * a_im
    num_re = lb_re - 1.0
    f_re = (num_re * a_re + lb_im * a_im) / den
    f_im = (lb_im * a_re - num_re * a_im) / den
    bbt_re = f_re[:, None, :] * jnp.swapaxes(b_re, 1, 2) - f_im[:, None, :] * jnp.swapaxes(b_im, 1, 2)
    bbt_im = f_re[:, None, :] * jnp.swapaxes(b_im, 1, 2) + f_im[:, None, :] * jnp.swapaxes(b_re, 1, 2)

    gp = min(g, 8)
    assert g % gp == 0
    blk = lambda *s: pl.BlockSpec((gp,) + s, lambda i: (i, 0, 0))
    kt, bmat, cmat_t = pl.pallas_call(
        _s5_param_kernel,
        grid=(g // gp,),
        in_specs=[blk(ch, p)] * 6 + [blk(gc, p)] * 4,
        out_specs=[blk(gc, n), blk(n, 2 * p), blk(n, 2 * p)],
        out_shape=[jax.ShapeDtypeStruct((g, gc, n), F32), jax.ShapeDtypeStruct((g, n, 2 * p), BF16),
                   jax.ShapeDtypeStruct((g, n, 2 * p), BF16)],
        compiler_params=_compiler_params(("parallel",)),
        name="s5_params",
    )(zr[:, :ch], zi[:, :ch], zr[:, 1:], zi[:, 1:], zr[:, ch - 1::-1][:, :ch], zi[:, ch - 1::-1][:, :ch],
      c_re, c_im, bbt_re, bbt_im)

    kt_pad = jnp.pad(kt, ((0, 0), (0, 0), (n, 0)))
    toep = jnp.stack([kt_pad[:, :, n - s * gc:2 * n - s * gc] for s in range(ch)], axis=1)
    toep = toep.reshape(g, n, n).astype(BF16)
    zl_r, zl_i = zr[:, ch], zi[:, ch]
    a1 = jnp.concatenate([zl_r, zl_r], axis=-1)
    a2 = jnp.concatenate([-zl_i, zl_i], axis=-1)
    a2s = jnp.concatenate([zl_i, -zl_i], axis=-1)
    return toep, bmat, cmat_t, a1, a2, a2s


GROUPS_PER_TILE = LANES // SSM_GROUP
TILES_PER_CHUNK_ROW = SSM_CHUNK * SSM_GROUP // LANES
RELAYOUT_ROWS = 16


def _lane_block_ids(rows):
    return lax.broadcasted_iota(jnp.int32, (rows, LANES), 1) // SSM_GROUP


def _transpose_granules(v, blk_id):
    v = list(v)
    d = GROUPS_PER_TILE // 2
    while d >= 1:
        low = (blk_id & d) == 0
        nxt = list(v)
        for j in range(GROUPS_PER_TILE):
            if j & d:
                continue
            a, b = v[j], v[j + d]
            nxt[j] = jnp.where(low, a, pltpu.roll(b, d * SSM_GROUP, axis=1))
            nxt[j + d] = jnp.where(low, pltpu.roll(a, LANES - d * SSM_GROUP, axis=1), b)
        v = nxt
        d //= 2
    return v


def _s5_state_kernel(x_ref, bm_ref, u_ref, e_ref, xf_ref):
    ch, gc, rb = SSM_CHUNK, SSM_GROUP, RELAYOUT_ROWS
    n_chunks = u_ref.shape[1]
    xf_ref[...] = x_ref[...].astype(F32)
    blk_id = _lane_block_ids(rb)

    def body(i, carry):
        row0 = pl.multiple_of(i * rb, rb)
        tok = [xf_ref[pl.ds(row0 * ch + s, rb, stride=ch), :].astype(BF16) for s in range(ch)]
        for hf in range(TILES_PER_CHUNK_ROW):
            grp = _transpose_granules(tok[hf * GROUPS_PER_TILE:(hf + 1) * GROUPS_PER_TILE], blk_id)
            for g in range(GROUPS_PER_TILE):
                u_ref[g, pl.ds(row0, rb), hf * LANES:(hf + 1) * LANES] = grp[g]
        return carry

    lax.fori_loop(0, n_chunks // rb, body, 0, unroll=4)
    nc = e_ref.shape[1]
    for g in range(GROUPS_PER_TILE):
        e = jnp.dot(u_ref[g], bm_ref[g], preferred_element_type=F32)
        for b in range(e_ref.shape[0]):
            e_ref[b, :, g, :] = e[b * nc:(b + 1) * nc]


def _s5_scan_kernel(e_ref, a1_ref, a2_ref, a2s_ref, x0_ref):
    a1, a2, a2s = a1_ref[...], a2_ref[...], a2s_ref[...]
    half = a1.shape[-1] // 2

    def body(k, carry):
        s, sw = carry
        x0_ref[0, k] = s
        e = e_ref[0, k]
        e_sw = pltpu.roll(e, half, axis=1)
        return a1 * s + a2 * sw + e, a1 * sw + a2s * s + e_sw

    zero = jnp.zeros(a1.shape, F32)
    lax.fori_loop(0, e_ref.shape[1], body, (zero, zero), unroll=8)


def _s5_out_kernel(u_ref, toep_ref, x0_ref, cmt_ref, d_ref, z_ref, y_ref, zf_ref):
    ch, gc, rb = SSM_CHUNK, SSM_GROUP, RELAYOUT_ROWS
    n_chunks = u_ref.shape[1]
    for g in range(GROUPS_PER_TILE):
        u = u_ref[g]
        x0 = jnp.concatenate([x0_ref[b, :, g, :] for b in range(x0_ref.shape[0])], axis=0).astype(BF16)
        y = (jnp.dot(u, toep_ref[g], preferred_element_type=F32)
             + lax.dot_general(x0, cmt_ref[g], _NT, preferred_element_type=F32)
             + d_ref[g] * u.astype(F32))
        y_ref[g] = jax.nn.gelu(y).astype(BF16)
    blk_id = _lane_block_ids(rb)

    def body(i, carry):
        row0 = pl.multiple_of(i * rb, rb)
        for hf in range(TILES_PER_CHUNK_ROW):
            grp = [y_ref[g, pl.ds(row0, rb), hf * LANES:(hf + 1) * LANES] for g in range(GROUPS_PER_TILE)]
            tok = _transpose_granules(grp, blk_id)
            for j in range(GROUPS_PER_TILE):
                s = hf * GROUPS_PER_TILE + j
                zf_ref[pl.ds(row0 * ch + s, rb, stride=ch), :] = tok[j].astype(F32)
        return carry

    lax.fori_loop(0, n_chunks // rb, body, 0, unroll=4)
    z_ref[...] = zf_ref[...].astype(BF16)


def _s5_core(hn, bsz, seq, ssm, d_skip):
    toep, bmat, cmat, a1, a2, a2s = ssm
    t, d = hn.shape
    g = d // SSM_GROUP
    gc, ch = SSM_GROUP, SSM_CHUNK
    n = gc * ch
    nc = seq // ch
    r = bsz * nc
    p2 = bmat.shape[-1]
    gb = GROUPS_PER_TILE
    assert r % RELAYOUT_ROWS == 0 and g % gb == 0

    u_g, e = pl.pallas_call(
        _s5_state_kernel,
        grid=(g // gb,),
        in_specs=[pl.BlockSpec((t, LANES), lambda i: (0, i)), pl.BlockSpec((gb, n, p2), lambda i: (i, 0, 0))],
        out_specs=[pl.BlockSpec((gb, r, n), lambda i: (i, 0, 0)),
                   pl.BlockSpec((bsz, nc, gb, p2), lambda i: (0, 0, i, 0))],
        out_shape=[jax.ShapeDtypeStruct((g, r, n), BF16), jax.ShapeDtypeStruct((bsz, nc, g, p2), F32)],
        scratch_shapes=[pltpu.VMEM((t, LANES), F32)],
        compiler_params=_compiler_params(("parallel",)),
        name="s5_chunk_state",
    )(hn, bmat)

    gs = min(g, 32)
    tab = pl.BlockSpec((gs, p2), lambda b, i: (i, 0))
    x0 = pl.pallas_call(
        _s5_scan_kernel,
        grid=(bsz, g // gs),
        in_specs=[pl.BlockSpec((1, nc, gs, p2), lambda b, i: (b, 0, i, 0)), tab, tab, tab],
        out_specs=pl.BlockSpec((1, nc, gs, p2), lambda b, i: (b, 0, i, 0)),
        out_shape=jax.ShapeDtypeStruct((bsz, nc, g, p2), F32),
        compiler_params=_compiler_params(("parallel", "parallel")),
        name="s5_chunk_scan",
    )(e, a1, a2, a2s)

    d_tile = jnp.tile(d_skip.astype(F32).reshape(g, 1, gc), (1, 1, ch))
    return pl.pallas_call(
        _s5_out_kernel,
        grid=(g // gb,),
        in_specs=[pl.BlockSpec((gb, r, n), lambda i: (i, 0, 0)), pl.BlockSpec((gb, n, n), lambda i: (i, 0, 0)),
                  pl.BlockSpec((bsz, nc, gb, p2), lambda i: (0, 0, i, 0)),
                  pl.BlockSpec((gb, n, p2), lambda i: (i, 0, 0)), pl.BlockSpec((gb, 1, n), lambda i: (i, 0, 0))],
        out_specs=pl.BlockSpec((t, LANES), lambda i: (0, i)),
        out_shape=jax.ShapeDtypeStruct((t, d), BF16),
        scratch_shapes=[pltpu.VMEM((gb, r, n), BF16), pltpu.VMEM((t, LANES), F32)],
        compiler_params=_compiler_params(("parallel",)),
        name="s5_chunk_out",
    )(u_g, toep, x0, cmat, d_tile)


def _glu_kernel(z_ref, h_ref, w1_ref, w2_ref, b_ref, g_ref, o_ref):
    z = z_ref[...]
    lin = jnp.dot(z, w1_ref[...], preferred_element_type=F32)
    gate = jnp.dot(z, w2_ref[...], preferred_element_type=F32) + b_ref[...]
    o_ref[...] = h_ref[...] + _rms_norm(lin * jax.nn.sigmoid(gate), g_ref[...])


def _glu(z, h, w_out, w_gate, b_gate, g_post, tm=512):
    t, d = h.shape
    tm = _row_tile(t, tm)
    row = pl.BlockSpec((tm, d), lambda i: (i, 0))
    return pl.pallas_call(
        _glu_kernel,
        grid=(t // tm,),
        in_specs=[row, row, _resident((d, d)), _resident((d, d)), _resident((1, d)), _resident((1, d))],
        out_specs=row,
        out_shape=jax.ShapeDtypeStruct((t, d), F32),
        compiler_params=_compiler_params(("parallel",)),
        name="s5_glu",
    )(z, h, w_out, w_gate, b_gate.reshape(1, d), g_post.reshape(1, d))


def _rope_tables(seq):
    half = ROPE_DIM // 2
    inv_freq = ROPE_THETA ** (-jnp.arange(half, dtype=F32) / half)
    ang = jnp.arange(seq, dtype=F32)[:, None] * inv_freq[None, :]
    cos, sin = jnp.cos(ang), jnp.sin(ang)
    ones = jnp.ones((seq, HEAD_DIM - ROPE_DIM), F32)
    zeros = jnp.zeros((seq, HEAD_DIM - half), F32)
    cos_h = jnp.concatenate([cos, cos, ones], axis=1)
    sin_up = jnp.concatenate([-sin, zeros], axis=1)
    sin_dn = jnp.concatenate([jnp.zeros((seq, half), F32), sin, jnp.zeros((seq, HEAD_DIM - ROPE_DIM), F32)], axis=1)
    rep = LANES // HEAD_DIM
    lane_tabs = (jnp.tile(cos_h, (1, rep)), jnp.tile(sin_up, (1, rep)), jnp.tile(sin_dn, (1, rep)))
    return lane_tabs, (cos.T, sin.T)


def _rope_lanes(x, cos, sin_up, sin_dn):
    half = ROPE_DIM // 2
    return (x * cos + pltpu.roll(x, LANES - half, axis=1) * sin_up + pltpu.roll(x, half, axis=1) * sin_dn)


def _kv_kernel(hn_ref, wk_ref, bk_ref, wvt_ref, bvt_ref, cos_ref, sup_ref, sdn_ref, k_ref, vt_ref):
    hn = hn_ref[...]
    k = jnp.dot(hn, wk_ref[...], preferred_element_type=F32) + bk_ref[...]
    cos, sup, sdn = cos_ref[...], sup_ref[...], sdn_ref[...]
    heads_per_tile = LANES // HEAD_DIM
    for c in range(k.shape[-1] // LANES):
        kr = _rope_lanes(k[:, c * LANES:(c + 1) * LANES], cos, sup, sdn).astype(BF16)
        for e in range(heads_per_tile):
            k_ref[c * heads_per_tile + e] = kr[:, e * HEAD_DIM:(e + 1) * HEAD_DIM]
    vt = lax.dot_general(wvt_ref[...], hn, _NT, preferred_element_type=F32) + bvt_ref[...]
    vt_ref[...] = vt.astype(BF16)


def _shared_kv(hn, seq, w_kv, b_kv, lane_tabs, tm=512):
    t, d = hn.shape
    kw = w_kv.shape[1] // 2
    n_kv = kw // HEAD_DIM
    assert kw % LANES == 0
    tm = _row_tile(seq, tm)
    row = pl.BlockSpec((tm, d), lambda i: (i, 0))
    tab = pl.BlockSpec((tm, LANES), lambda i: (i % (seq // tm), 0))
    wk = w_kv[:, :kw].astype(BF16)
    wvt = w_kv[:, kw:].T.astype(BF16)
    return pl.pallas_call(
        _kv_kernel,
        grid=(t // tm,),
        in_specs=[row, _resident((d, kw)), _resident((1, kw)), _resident((kw, d)), _resident((kw, 1)),
                  tab, tab, tab],
        out_specs=[pl.BlockSpec((n_kv, tm, HEAD_DIM), lambda i: (0, i, 0)), pl.BlockSpec((kw, tm), lambda i: (0, i))],
        out_shape=[jax.ShapeDtypeStruct((n_kv, t, HEAD_DIM), BF16), jax.ShapeDtypeStruct((kw, t), BF16)],
        compiler_params=_compiler_params(("parallel",)),
        name="shared_kv",
    )(hn, wk, b_kv[:kw].reshape(1, kw), wvt, b_kv[kw:].reshape(kw, 1), *lane_tabs)


def _q_kernel(hn_ref, wt_ref, bt_ref, cos_ref, sin_ref, q_ref):
    scale = HEAD_DIM ** -0.5 * LOG2_E
    qt =(lax.dot_general(wt_ref[...], hn_ref[...], _NT, preferred_element_type=F32) + bt_ref[...]) * scale
    q_ref[...] = qt.astype(BF16)
    cos, sin = cos_ref[...], sin_ref[...]
    half = ROPE_DIM // 2
    for head in range(qt.shape[0] // HEAD_DIM):
        r0 = head * HEAD_DIM
        t1, t2 = qt[r0:r0 + half], qt[r0 + half:r0 + ROPE_DIM]
        rot = jnp.concatenate([t1 * cos - t2 * sin, t2 * cos + t1 * sin], axis=0)
        q_ref[r0:r0 + ROPE_DIM, :] = rot.astype(BF16)


def _q_proj(hn, seq, w_q, b_q, freq_tabs, tm=512):
    t, d = hn.shape
    tm = _row_tile(seq, tm)
    half = ROPE_DIM // 2
    tab = pl.BlockSpec((half, tm), lambda i: (0, i % (seq // tm)))
    return pl.pallas_call(
        _q_kernel,
        grid=(t // tm,),
        in_specs=[pl.BlockSpec((tm, d), lambda i: (i, 0)), _resident((d, d)), _resident((d, 1)), tab, tab],
        out_specs=pl.BlockSpec((d, tm), lambda i: (0, i)),
        out_shape=jax.ShapeDtypeStruct((d, t), BF16),
        compiler_params=_compiler_params(("parallel",)),
        name="q_proj",
    )(hn, w_q.T.astype(BF16), b_q.reshape(d, 1), *freq_tabs)


def _attend_tile(n_kv, qb, step_in_seq, sink_ref, q_ref, kc_ref, kp_ref, vc_ref, vp_ref, o_ref, interleave=None):
    blk = ATTN_BLOCK
    width = Q_PER_KV * blk
    r = lax.broadcasted_iota(jnp.int32, (blk, width), 0)
    qi = lax.broadcasted_iota(jnp.int32, (blk, width), 1) % blk
    from_prev = r > qi
    has_prev = (step_in_seq > 0) | (r < 0)
    zero = jnp.zeros((blk, width), BF16)
    ones_rows = jnp.ones((8, 2 * blk), BF16)
    for j in range(qb):
        lanes = slice(j * blk, (j + 1) * blk)
        for kvh in range(n_kv):
            rows = slice(kvh * HEAD_DIM, (kvh + 1) * HEAD_DIM)
            if j == 0:
                kb = jnp.concatenate([kp_ref[kvh], kc_ref[kvh, :blk]], axis=0)
                vbt = jnp.concatenate([vp_ref[rows, :], vc_ref[rows, :blk]], axis=1)
            else:
                kb = kc_ref[kvh, (j - 1) * blk:(j + 1) * blk]
                vbt = vc_ref[rows, (j - 1) * blk:(j + 1) * blk]
            qcat = jnp.concatenate(
                [q_ref[(kvh * Q_PER_KV + gq) * HEAD_DIM:(kvh * Q_PER_KV + gq + 1) * HEAD_DIM, lanes]
                 for gq in range(Q_PER_KV)], axis=1)
            s2 = jnp.dot(kb, qcat, preferred_element_type=F32)
            s_prev = s2[:blk]
            if j == 0:
                s_prev = jnp.where(has_prev, s_prev, MASK_VALUE)
            s = jnp.where(from_prev, s_prev, s2[blk:])
            sink = sink_ref[kvh]
            m = jnp.maximum(jnp.max(s, axis=0, keepdims=True), sink)
            p = jnp.exp2(s - m).astype(BF16)
            p2 = jnp.concatenate([jnp.where(from_prev, p, zero), jnp.where(from_prev, zero, p)], axis=0)
            ov = jnp.dot(jnp.concatenate([vbt, ones_rows], axis=0), p2, preferred_element_type=F32)
            den = ov[HEAD_DIM:HEAD_DIM + 1] + jnp.exp2(sink - m)
            o = ov[:HEAD_DIM] * (1.0 / den)
            for gq in range(Q_PER_KV):
                r0 = (kvh * Q_PER_KV + gq) * HEAD_DIM
                o_ref[r0:r0 + HEAD_DIM, lanes] = o[:, gq * blk:(gq + 1) * blk].astype(BF16)
            if interleave is not None:
                interleave(j * n_kv + kvh, qb * n_kv)


def _attn_o_kernel(n_kv, qb, steps_per_seq, sink_ref, q_ref, kc_ref, kp_ref, vc_ref, vp_ref, h_ref, w_ref, b_ref,
                   g_ref, out_ref, at_even_ref, at_odd_ref, mix_ref):
    s = pl.program_id(0)
    tile = jnp.minimum(s, pl.num_programs(0) - 2)

    @pl.when(s == 0)
    def _():
        at_odd_ref[...] = jnp.zeros_like(at_odd_ref)

    def stage(done_ref, next_ref):
        d = w_ref.shape[1]
        a_prev = done_ref[...].T

        def project_chunk(i, n):
            if i % 2 == 0:
                return
            width = d // (n // 2)
            cols = slice((i // 2) * width, (i // 2 + 1) * width)
            mix_ref[:, cols] = jnp.dot(a_prev, w_ref[:, cols], preferred_element_type=F32) + b_ref[:, cols]

        _attend_tile(n_kv, qb, tile % steps_per_seq, sink_ref, q_ref, kc_ref, kp_ref, vc_ref, vp_ref, next_ref,
                     interleave=project_chunk)
        out_ref[...] = h_ref[...] + _rms_norm(mix_ref[...], g_ref[...])

    @pl.when(s % 2 == 0)
    def _():
        stage(at_odd_ref, at_even_ref)

    @pl.when(s % 2 == 1)
    def _():
        stage(at_even_ref, at_odd_ref)


def _attention_o_proj(qt, k, vt, sinks, h, w_o, b_o, g_post, bsz, seq, qb=4):
    d, t = qt.shape
    n_kv = k.shape[0]
    blk = ATTN_BLOCK
    qb = min(qb, seq // blk)
    assert seq % (qb * blk) == 0
    tq = qb * blk
    steps = seq // tq
    n_tiles = bsz * steps
    sink_x = jnp.repeat(sinks.astype(F32) * LOG2_E, blk).reshape(n_kv, 1, Q_PER_KV * blk)
    cur = lambda s: jnp.minimum(s, n_tiles - 1)
    prev = lambda s: (cur(s) // steps) * steps * qb + jnp.maximum((cur(s) % steps) * qb - 1, 0)
    done = lambda s: jnp.maximum(s - 1, 0)
    return pl.pallas_call(
        functools.partial(_attn_o_kernel, n_kv, qb, steps),
        grid=(n_tiles + 1,),
        in_specs=[_resident((n_kv, 1, Q_PER_KV * blk)),
                  pl.BlockSpec((d, tq), lambda s: (0, cur(s))),
                  pl.BlockSpec((n_kv, tq, HEAD_DIM), lambda s: (0, cur(s), 0)),
                  pl.BlockSpec((n_kv, blk, HEAD_DIM), lambda s: (0, prev(s), 0)),
                  pl.BlockSpec((n_kv * HEAD_DIM, tq), lambda s: (0, cur(s))),
                  pl.BlockSpec((n_kv * HEAD_DIM, blk), lambda s: (0, prev(s))),
                  pl.BlockSpec((tq, d), lambda s: (done(s), 0)),
                  _resident((d, d)), _resident((1, d)), _resident((1, d))],
        out_specs=pl.BlockSpec((tq, d), lambda s: (done(s), 0)),
        out_shape=jax.ShapeDtypeStruct((t, d), F32),
        scratch_shapes=[pltpu.VMEM((d, tq), BF16), pltpu.VMEM((d, tq), BF16), pltpu.VMEM((tq, d), F32)],
        compiler_params=_compiler_params(("arbitrary",)),
        name="swa_sink_attention_o_proj",
    )(sink_x, qt, k, k, vt, vt, h, w_o.astype(BF16), b_o.reshape(1, d), g_post.reshape(1, d))


def kernel(x, norm_g, ffn_w_gate, ffn_w_up, ffn_w_down, ssm_a_re, ssm_a_im, ssm_log_dt, ssm_b_re, ssm_b_im, ssm_c_re, ssm_c_im, ssm_d, glu_w_out, glu_w_gate, glu_b_gate, kv_norm_g, w_kv, b_kv, w_q, b_q, attn_sinks, w_o, b_o):
    bsz, seq, d = x.shape
    depth = norm_g.shape[0]
    n_a = ssm_a_re.shape[0]
    assert seq % ATTN_BLOCK == 0 and seq % SSM_CHUNK == 0 and d % LANES == 0
    h = x.astype(F32).reshape(bsz * seq, d)
    lane_tabs, freq_tabs = _rope_tables(seq)
    ffn_order = [(layer, which) for layer in range(depth) for which in range(2)]
    weights = tuple(w[0, 0].astype(BF16) for w in (ffn_w_gate, ffn_w_up, ffn_w_down))

    def following(layer, which):
        pos = ffn_order.index((layer, which)) + 1
        return (ffn_w_gate, ffn_w_up, ffn_w_down) + ffn_order[pos] if pos < len(ffn_order) else None

    k = vt = None
    for layer in range(depth):
        g = norm_g[layer].astype(F32)
        h, hn, weights = _ffn(h, g[0], g[1], weights, g_next=g[2], next_weights=following(layer, 0))
        if layer < n_a:
            ssm = _s5_params(ssm_a_re[layer], ssm_a_im[layer], ssm_log_dt[layer], ssm_b_re[layer].astype(F32),
                             ssm_b_im[layer].astype(F32), ssm_c_re[layer].astype(F32), ssm_c_im[layer].astype(F32))
            z = _s5_core(hn, bsz, seq, ssm, ssm_d[layer])
            h = _glu(z, h, glu_w_out[layer].astype(BF16), glu_w_gate[layer].astype(BF16), glu_b_gate[layer], g[3])
        else:
            bl = layer - n_a
            qt = _q_proj(hn, seq, w_q[bl], b_q[bl], freq_tabs)
            h = _attention_o_proj(qt, k, vt, attn_sinks[bl], h, w_o[bl], b_o[bl], g[3], bsz, seq)
        g_kv = kv_norm_g.astype(F32) if layer == n_a - 1 else None
        h, hn_kv, weights = _ffn(h, g[4], g[5], weights, g_next=g_kv, next_weights=following(layer, 1))
        if layer == n_a - 1:
            k, vt = _shared_kv(hn_kv, seq, w_kv, b_kv, lane_tabs)
    return h.reshape(bsz, seq, d).astype(x.dtype)
```

```python
import functools
import math

import jax
import jax.numpy as jnp
from jax import lax
from jax.experimental import pallas as pl
from jax.experimental.pallas import tpu as pltpu

F32 = jnp.float32
BF16 = jnp.bfloat16

NORM_EPS = 1e-6
FFN_RESIDUAL_WEIGHT = 0.5
SSM_GROUP = 16
SSM_CHUNK = 16
HEAD_DIM = 64
Q_PER_KV = 8
ATTN_BLOCK = 128
ROPE_DIM = HEAD_DIM // 4
ROPE_THETA = 500000.0
MASK_VALUE = -1e30
LOG2_E = math.log2(math.e)
LANES = 128
VMEM_LIMIT_BYTES = 63 * 1024 * 1024


def _compiler_params(semantics):
    return pltpu.CompilerParams(dimension_semantics=semantics, vmem_limit_bytes=VMEM_LIMIT_BYTES)


def _rms_norm(x, g):
    return x * lax.rsqrt(jnp.mean(x * x, axis=-1, keepdims=True) + NORM_EPS) * g


def _resident(shape):
    return pl.BlockSpec(shape, lambda *_: (0,) * len(shape), pipeline_mode=pl.Buffered(1))


def _row_tile(t, want):
    tm = min(t, want)
    assert t % tm == 0
    return tm


def _ffn_kernel(emit_next, cast_next, h_ref, gpre_ref, gpost_ref, *rest):
    rest = list(rest)
    gnext_ref = rest.pop(0) if emit_next else None
    wg_ref, wu_ref, wd_ref = rest[:3]
    rest = rest[3:]
    next_f32 = [rest.pop(0) for _ in range(3)] if cast_next else []
    o_ref = rest.pop(0)
    on_ref = rest.pop(0) if emit_next else None
    next_bf16 = [rest.pop(0) for _ in range(3)] if cast_next else []
    xn_ref, acc_ref = rest
    j = pl.program_id(1)

    for src, dst in zip(next_f32, next_bf16):
        dst[...] = src[...].astype(BF16)

    def swiglu_chunk():
        xn = xn_ref[...]
        gate = jnp.dot(xn, wg_ref[...], preferred_element_type=F32)
        up = jnp.dot(xn, wu_ref[...], preferred_element_type=F32)
        act = (gate * jax.nn.sigmoid(gate) * up).astype(BF16)
        return jnp.dot(act, wd_ref[...], preferred_element_type=F32)

    @pl.when(j == 0)
    def _():
        xn_ref[...] = _rms_norm(h_ref[...], gpre_ref[...]).astype(BF16)
        acc_ref[...] = swiglu_chunk()

    last = pl.num_programs(1) - 1

    @pl.when((j > 0) & (j < last))
    def _():
        acc_ref[...] += swiglu_chunk()

    @pl.when(j == last)
    def _():
        acc_ref[...] += swiglu_chunk()
        h_new = h_ref[...] + _rms_norm(acc_ref[...], gpost_ref[...])
        o_ref[...] = h_new
        if emit_next:
            on_ref[...] = _rms_norm(h_new, gnext_ref[...]).astype(BF16)


def _ffn(h, g_pre, g_post, weights, g_next=None, next_weights=None, tm=1024, tf=512):
    t, d = h.shape
    w_gate, w_up, w_down = weights
    f = w_gate.shape[-1]
    tm = _row_tile(t, tm)
    tf = _row_tile(f, tf)
    ni, nj = t // tm, f // tf
    emit_next = g_next is not None
    cast_next = next_weights is not None
    row = pl.BlockSpec((tm, d), lambda i, j: (i, 0))
    vec = pl.BlockSpec((1, d), lambda i, j: (0, 0))
    in_specs = [row, vec, vec] + ([vec] if emit_next else []) + [
        pl.BlockSpec((d, tf), lambda i, j: (0, j)),
        pl.BlockSpec((d, tf), lambda i, j: (0, j)),
        pl.BlockSpec((tf, d), lambda i, j: (j, 0)),
    ]
    args = [h, g_pre.reshape(1, d), (FFN_RESIDUAL_WEIGHT * g_post).reshape(1, d)] + (
        [g_next.reshape(1, d)] if emit_next else []) + [w_gate, w_up, w_down]
    out_shape = [jax.ShapeDtypeStruct((t, d), F32)] + ([jax.ShapeDtypeStruct((t, d), BF16)] if emit_next else [])
    out_row = pl.BlockSpec((tm, d), lambda i, j: (i, 0), pipeline_mode=pl.Buffered(1))
    out_specs = [out_row] + ([out_row] if emit_next else [])
    if cast_next:
        nwg, nwu, nwd, layer, which = next_weights
        dr = d // ni
        assert d % ni == 0 and dr % LANES == 0
        in_specs += [pl.BlockSpec((None, None, dr, tf), lambda i, j: (layer, which, i, j)),
                     pl.BlockSpec((None, None, dr, tf), lambda i, j: (layer, which, i, j)),
                     pl.BlockSpec((None, None, tf, dr), lambda i, j: (layer, which, j, i))]
        args += [nwg, nwu, nwd]
        out_specs += [pl.BlockSpec((dr, tf), lambda i, j: (i, j)), pl.BlockSpec((dr, tf), lambda i, j: (i, j)),
                      pl.BlockSpec((tf, dr), lambda i, j: (j, i))]
        out_shape += [jax.ShapeDtypeStruct((d, f), BF16), jax.ShapeDtypeStruct((d, f), BF16),
                      jax.ShapeDtypeStruct((f, d), BF16)]
    outs = pl.pallas_call(
        functools.partial(_ffn_kernel, emit_next, cast_next),
        grid=(ni, nj),
        in_specs=in_specs,
        out_specs=out_specs,
        out_shape=out_shape,
        scratch_shapes=[pltpu.VMEM((tm, d), BF16), pltpu.VMEM((tm, d), F32)],
        compiler_params=_compiler_params(("parallel", "arbitrary")),
        name="ffn",
    )(*args)
    h_new = outs[0]
    normed = outs[1] if emit_next else None
    casted = tuple(outs[-3:]) if cast_next else None
    return h_new, normed, casted


_NT = (((1,), (1,)), ((), ()))
_TN = (((0,), (0,)), ((), ()))


def _s5_param_kernel(z0r_ref, z0i_ref, z1r_ref, z1i_ref, zvr_ref, zvi_ref, cr_ref, ci_ref, bbr_ref, bbi_ref,
                     kt_ref, bm_ref, cmt_ref):
    rows, p = z0r_ref.shape[1], z0r_ref.shape[2]

    def rep(a):
        return jnp.broadcast_to(a[:, None, :], (rows, SSM_GROUP, p)).reshape(rows * SSM_GROUP, p)

    def til(a):
        return jnp.broadcast_to(a[None, :, :], (rows, SSM_GROUP, p)).reshape(rows * SSM_GROUP, p)

    for gi in range(z0r_ref.shape[0]):
        cr, ci = til(cr_ref[gi]), til(ci_ref[gi])
        z0r, z0i = rep(z0r_ref[gi]), rep(z0i_ref[gi])
        zc_r = z0r * cr - z0i * ci
        zc_i = z0r * ci + z0i * cr
        kt_ref[gi] = (
            lax.dot_general(bbr_ref[gi], zc_r, _NT, precision=lax.Precision.HIGHEST, preferred_element_type=F32)
            - lax.dot_general(bbi_ref[gi], zc_i, _NT, precision=lax.Precision.HIGHEST, preferred_element_type=F32))
        z1r, z1i = rep(z1r_ref[gi]), rep(z1i_ref[gi])
        cmt_ref[gi] = jnp.concatenate([z1r * cr - z1i * ci, -(z1r * ci + z1i * cr)], axis=1).astype(BF16)
        zvr, zvi = rep(zvr_ref[gi]), rep(zvi_ref[gi])
        btr, bti = til(bbr_ref[gi]), til(bbi_ref[gi])
        bm_ref[gi] = jnp.concatenate([zvr * btr - zvi * bti, zvr * bti + zvi * btr], axis=1).astype(BF16)


def _s5_params(a_re, a_im, log_dt, b_re, b_im, c_re, c_im):
    g, p = a_re.shape
    gc, ch = SSM_GROUP, SSM_CHUNK
    n = ch * gc
    dt = jnp.exp(log_dt.astype(F32))[:, None]
    lam_re = a_re.astype(F32) * dt
    lam_im = a_im.astype(F32) * dt
    lags = jnp.arange(ch + 1, dtype=F32)[None, :, None]
    mag = jnp.exp(lags * lam_re[:, None, :])
    zr = mag * jnp.cos(lags * lam_im[:, None, :])
    zi = mag * jnp.sin(lags * lam_im[:, None, :])
    lb_re, lb_im = zr[:, 1], zi[:, 1]
    den = a_re * a_re + a_im * a_im
    num_re = lb_re - 1.0
    f_re = (num_re * a_re + lb_im * a_im) / den
    f_im = (lb_im * a_re - num_re * a_im) / den
    bbt_re = f_re[:, None, :] * jnp.swapaxes(b_re, 1, 2) - f_im[:, None, :] * jnp.swapaxes(b_im, 1, 2)
    bbt_im = f_re[:, None, :] * jnp.swapaxes(b_im, 1, 2) + f_im[:, None, :] * jnp.swapaxes(b_re, 1, 2)

    gp = min(g, 8)
    assert g % gp == 0
    blk = lambda *s: pl.BlockSpec((gp,) + s, lambda i: (i, 0, 0))
    kt, bmat, cmat_t = pl.pallas_call(
        _s5_param_kernel,
        grid=(g // gp,),
        in_specs=[blk(ch, p)] * 6 + [blk(gc, p)] * 4,
        out_specs=[blk(gc, n), blk(n, 2 * p), blk(n, 2 * p)],
        out_shape=[jax.ShapeDtypeStruct((g, gc, n), F32), jax.ShapeDtypeStruct((g, n, 2 * p), BF16),
                   jax.ShapeDtypeStruct((g, n, 2 * p), BF16)],
        compiler_params=_compiler_params(("parallel",)),
        name="s5_params",
    )(zr[:, :ch], zi[:, :ch], zr[:, 1:], zi[:, 1:], zr[:, ch - 1::-1][:, :ch], zi[:, ch - 1::-1][:, :ch],
      c_re, c_im, bbt_re, bbt_im)

    kt_pad = jnp.pad(kt, ((0, 0), (0, 0), (n, 0)))
    toep = jnp.stack([kt_pad[:, :, n - s * gc:2 * n - s * gc] for s in range(ch)], axis=1)
    toep = toep.reshape(g, n, n).astype(BF16)
    zl_r, zl_i = zr[:, ch], zi[:, ch]
    a1 = jnp.concatenate([zl_r, zl_r], axis=-1)
    a2 = jnp.concatenate([-zl_i, zl_i], axis=-1)
    a2s = jnp.concatenate([zl_i, -zl_i], axis=-1)
    return toep, bmat, cmat_t, a1, a2, a2s


GROUPS_PER_TILE = LANES // SSM_GROUP
TILES_PER_CHUNK_ROW = SSM_CHUNK * SSM_GROUP // LANES
RELAYOUT_ROWS = 16


def _lane_block_ids(rows):
    return lax.broadcasted_iota(jnp.int32, (rows, LANES), 1) // SSM_GROUP


def _transpose_granules(v, blk_id):
    v = list(v)
    d = GROUPS_PER_TILE // 2
    while d >= 1:
        low = (blk_id & d) == 0
        nxt = list(v)
        for j in range(GROUPS_PER_TILE):
            if j & d:
                continue
            a, b = v[j], v[j + d]
            nxt[j] = jnp.where(low, a, pltpu.roll(b, d * SSM_GROUP, axis=1))
            nxt[j + d] = jnp.where(low, pltpu.roll(a, LANES - d * SSM_GROUP, axis=1), b)
        v = nxt
        d //= 2
    return v


def _s5_state_kernel(x_ref, bm_ref, u_ref, e_ref, xf_ref):
    ch, gc, rb = SSM_CHUNK, SSM_GROUP, RELAYOUT_ROWS
    n_chunks = u_ref.shape[1]
    xf_ref[...] = x_ref[...].astype(F32)
    blk_id = _lane_block_ids(rb)

    def body(i, carry):
        row0 = pl.multiple_of(i * rb, rb)
        tok = [xf_ref[pl.ds(row0 * ch + s, rb, stride=ch), :].astype(BF16) for s in range(ch)]
        for hf in range(TILES_PER_CHUNK_ROW):
            grp = _transpose_granules(tok[hf * GROUPS_PER_TILE:(hf + 1) * GROUPS_PER_TILE], blk_id)
            for g in range(GROUPS_PER_TILE):
                u_ref[g, pl.ds(row0, rb), hf * LANES:(hf + 1) * LANES] = grp[g]
        return carry

    lax.fori_loop(0, n_chunks // rb, body, 0, unroll=4)
    nc = e_ref.shape[1]
    for g in range(GROUPS_PER_TILE):
        e = jnp.dot(u_ref[g], bm_ref[g], preferred_element_type=F32)
        for b in range(e_ref.shape[0]):
            e_ref[b, :, g, :] = e[b * nc:(b + 1) * nc]


def _s5_scan_kernel(e_ref, a1_ref, a2_ref, a2s_ref, x0_ref):
    a1, a2, a2s = a1_ref[...], a2_ref[...], a2s_ref[...]
    half = a1.shape[-1] // 2

    def body(k, carry):
        s, sw = carry
        x0_ref[0, k] = s
        e = e_ref[0, k]
        e_sw = pltpu.roll(e, half, axis=1)
        return a1 * s + a2 * sw + e, a1 * sw + a2s * s + e_sw

    zero = jnp.zeros(a1.shape, F32)
    lax.fori_loop(0, e_ref.shape[1], body, (zero, zero), unroll=8)


def _s5_out_kernel(u_ref, toep_ref, x0_ref, cmt_ref, d_ref, z_ref, y_ref, zf_ref):
    ch, gc, rb = SSM_CHUNK, SSM_GROUP, RELAYOUT_ROWS
    n_chunks = u_ref.shape[1]
    for g in range(GROUPS_PER_TILE):
        u = u_ref[g]
        x0 = jnp.concatenate([x0_ref[b, :, g, :] for b in range(x0_ref.shape[0])], axis=0).astype(BF16)
        y = (jnp.dot(u, toep_ref[g], preferred_element_type=F32)
             + lax.dot_general(x0, cmt_ref[g], _NT, preferred_element_type=F32)
             + d_ref[g] * u.astype(F32))
        y_ref[g] = jax.nn.gelu(y).astype(BF16)
    blk_id = _lane_block_ids(rb)

    def body(i, carry):
        row0 = pl.multiple_of(i * rb, rb)
        for hf in range(TILES_PER_CHUNK_ROW):
            grp = [y_ref[g, pl.ds(row0, rb), hf * LANES:(hf + 1) * LANES] for g in range(GROUPS_PER_TILE)]
            tok = _transpose_granules(grp, blk_id)
            for j in range(GROUPS_PER_TILE):
                s = hf * GROUPS_PER_TILE + j
                zf_ref[pl.ds(row0 * ch + s, rb, stride=ch), :] = tok[j].astype(F32)
        return carry

    lax.fori_loop(0, n_chunks // rb, body, 0, unroll=4)
    z_ref[...] = zf_ref[...].astype(BF16)


def _s5_core(hn, bsz, seq, ssm, d_skip):
    toep, bmat, cmat, a1, a2, a2s = ssm
    t, d = hn.shape
    g = d // SSM_GROUP
    gc, ch = SSM_GROUP, SSM_CHUNK
    n = gc * ch
    nc = seq // ch
    r = bsz * nc
    p2 = bmat.shape[-1]
    gb = GROUPS_PER_TILE
    assert r % RELAYOUT_ROWS == 0 and g % gb == 0

    u_g, e = pl.pallas_call(
        _s5_state_kernel,
        grid=(g // gb,),
        in_specs=[pl.BlockSpec((t, LANES), lambda i: (0, i)), pl.BlockSpec((gb, n, p2), lambda i: (i, 0, 0))],
        out_specs=[pl.BlockSpec((gb, r, n), lambda i: (i, 0, 0)),
                   pl.BlockSpec((bsz, nc, gb, p2), lambda i: (0, 0, i, 0))],
        out_shape=[jax.ShapeDtypeStruct((g, r, n), BF16), jax.ShapeDtypeStruct((bsz, nc, g, p2), F32)],
        scratch_shapes=[pltpu.VMEM((t, LANES), F32)],
        compiler_params=_compiler_params(("parallel",)),
        name="s5_chunk_state",
    )(hn, bmat)

    gs = min(g, 32)
    tab = pl.BlockSpec((gs, p2), lambda b, i: (i, 0))
    x0 = pl.pallas_call(
        _s5_scan_kernel,
        grid=(bsz, g // gs),
        in_specs=[pl.BlockSpec((1, nc, gs, p2), lambda b, i: (b, 0, i, 0)), tab, tab, tab],
        out_specs=pl.BlockSpec((1, nc, gs, p2), lambda b, i: (b, 0, i, 0)),
        out_shape=jax.ShapeDtypeStruct((bsz, nc, g, p2), F32),
        compiler_params=_compiler_params(("parallel", "parallel")),
        name="s5_chunk_scan",
    )(e, a1, a2, a2s)

    d_tile = jnp.tile(d_skip.astype(F32).reshape(g, 1, gc), (1, 1, ch))
    return pl.pallas_call(
        _s5_out_kernel,
        grid=(g // gb,),
        in_specs=[pl.BlockSpec((gb, r, n), lambda i: (i, 0, 0)), pl.BlockSpec((gb, n, n), lambda i: (i, 0, 0)),
                  pl.BlockSpec((bsz, nc, gb, p2), lambda i: (0, 0, i, 0)),
                  pl.BlockSpec((gb, n, p2), lambda i: (i, 0, 0)), pl.BlockSpec((gb, 1, n), lambda i: (i, 0, 0))],
        out_specs=pl.BlockSpec((t, LANES), lambda i: (0, i)),
        out_shape=jax.ShapeDtypeStruct((t, d), BF16),
        scratch_shapes=[pltpu.VMEM((gb, r, n), BF16), pltpu.VMEM((t, LANES), F32)],
        compiler_params=_compiler_params(("parallel",)),
        name="s5_chunk_out",
    )(u_g, toep, x0, cmat, d_tile)


def _glu_kernel(z_ref, h_ref, w1_ref, w2_ref, b_ref, g_ref, o_ref):
    z = z_ref[...]
    lin = jnp.dot(z, w1_ref[...], preferred_element_type=F32)
    gate = jnp.dot(z, w2_ref[...], preferred_element_type=F32) + b_ref[...]
    o_ref[...] = h_ref[...] + _rms_norm(lin * jax.nn.sigmoid(gate), g_ref[...])


def _glu(z, h, w_out, w_gate, b_gate, g_post, tm=512):
    t, d = h.shape
    tm = _row_tile(t, tm)
    row = pl.BlockSpec((tm, d), lambda i: (i, 0))
    return pl.pallas_call(
        _glu_kernel,
        grid=(t // tm,),
        in_specs=[row, row, _resident((d, d)), _resident((d, d)), _resident((1, d)), _resident((1, d))],
        out_specs=row,
        out_shape=jax.ShapeDtypeStruct((t, d), F32),
        compiler_params=_compiler_params(("parallel",)),
        name="s5_glu",
    )(z, h, w_out, w_gate, b_gate.reshape(1, d), g_post.reshape(1, d))


def _rope_tables(seq):
    half = ROPE_DIM // 2
    inv_freq = ROPE_THETA ** (-jnp.arange(half, dtype=F32) / half)
    ang = jnp.arange(seq, dtype=F32)[:, None] * inv_freq[None, :]
    cos, sin = jnp.cos(ang), jnp.sin(ang)
    ones = jnp.ones((seq, HEAD_DIM - ROPE_DIM), F32)
    zeros = jnp.zeros((seq, HEAD_DIM - half), F32)
    cos_h = jnp.concatenate([cos, cos, ones], axis=1)
    sin_up = jnp.concatenate([-sin, zeros], axis=1)
    sin_dn = jnp.concatenate([jnp.zeros((seq, half), F32), sin, jnp.zeros((seq, HEAD_DIM - ROPE_DIM), F32)], axis=1)
    rep = LANES // HEAD_DIM
    lane_tabs = (jnp.tile(cos_h, (1, rep)), jnp.tile(sin_up, (1, rep)), jnp.tile(sin_dn, (1, rep)))
    return lane_tabs, (cos.T, sin.T)


def _rope_lanes(x, cos, sin_up, sin_dn):
    half = ROPE_DIM // 2
    return (x * cos + pltpu.roll(x, LANES - half, axis=1) * sin_up + pltpu.roll(x, half, axis=1) * sin_dn)


def _kv_kernel(hn_ref, wk_ref, bk_ref, wvt_ref, bvt_ref, cos_ref, sup_ref, sdn_ref, k_ref, vt_ref):
    hn = hn_ref[...]
    k = jnp.dot(hn, wk_ref[...], preferred_element_type=F32) + bk_ref[...]
    cos, sup, sdn = cos_ref[...], sup_ref[...], sdn_ref[...]
    heads_per_tile = LANES // HEAD_DIM
    for c in range(k.shape[-1] // LANES):
        kr = _rope_lanes(k[:, c * LANES:(c + 1) * LANES], cos, sup, sdn).astype(BF16)
        for e in range(heads_per_tile):
            k_ref[c * heads_per_tile + e] = kr[:, e * HEAD_DIM:(e + 1) * HEAD_DIM]
    vt = lax.dot_general(wvt_ref[...], hn, _NT, preferred_element_type=F32) + bvt_ref[...]
    vt_ref[...] = vt.astype(BF16)


def _shared_kv(hn, seq, w_kv, b_kv, lane_tabs, tm=512):
    t, d = hn.shape
    kw = w_kv.shape[1] // 2
    n_kv = kw // HEAD_DIM
    assert kw % LANES == 0
    tm = _row_tile(seq, tm)
    row = pl.BlockSpec((tm, d), lambda i: (i, 0))
    tab = pl.BlockSpec((tm, LANES), lambda i: (i % (seq // tm), 0))
    wk = w_kv[:, :kw].astype(BF16)
    wvt = w_kv[:, kw:].T.astype(BF16)
    return pl.pallas_call(
        _kv_kernel,
        grid=(t // tm,),
        in_specs=[row, _resident((d, kw)), _resident((1, kw)), _resident((kw, d)), _resident((kw, 1)),
                  tab, tab, tab],
        out_specs=[pl.BlockSpec((n_kv, tm, HEAD_DIM), lambda i: (0, i, 0)), pl.BlockSpec((kw, tm), lambda i: (0, i))],
        out_shape=[jax.ShapeDtypeStruct((n_kv, t, HEAD_DIM), BF16), jax.ShapeDtypeStruct((kw, t), BF16)],
        compiler_params=_compiler_params(("parallel",)),
        name="shared_kv",
    )(hn, wk, b_kv[:kw].reshape(1, kw), wvt, b_kv[kw:].reshape(kw, 1), *lane_tabs)


def _project_q(hn, w_ref, b_ref, cos, sin, q_ref, feats):
    scale = HEAD_DIM ** -0.5 * LOG2_E
    q = (jnp.dot(hn, w_ref[:, feats], preferred_element_type=F32) + b_ref[:, feats]) * scale
    qt = q.T
    q_ref[feats, :] = qt.astype(BF16)
    half = ROPE_DIM // 2
    for head in range(qt.shape[0] // HEAD_DIM):
        r0 = head * HEAD_DIM
        t1, t2 = qt[r0:r0 + half], qt[r0 + half:r0 + ROPE_DIM]
        rot = jnp.concatenate([t1 * cos - t2 * sin, t2 * cos + t1 * sin], axis=0)
        q_ref[feats.start + r0:feats.start + r0 + ROPE_DIM, :] = rot.astype(BF16)


def _attend_tile(n_kv, qb, step_in_seq, sink_ref, q_ref, kc_ref, kp_ref, vc_ref, vp_ref, o_ref, interleave=None):
    blk = ATTN_BLOCK
    width = Q_PER_KV * blk
    r = lax.broadcasted_iota(jnp.int32, (blk, width), 0)
    qi = lax.broadcasted_iota(jnp.int32, (blk, width), 1) % blk
    from_prev = r > qi
    has_prev = (step_in_seq > 0) | (r < 0)
    zero = jnp.zeros((blk, width), BF16)
    ones_rows = jnp.ones((8, 2 * blk), BF16)
    for j in range(qb):
        lanes = slice(j * blk, (j + 1) * blk)
        for kvh in range(n_kv):
            rows = slice(kvh * HEAD_DIM, (kvh + 1) * HEAD_DIM)
            if j == 0:
                kb = jnp.concatenate([kp_ref[kvh], kc_ref[kvh, :blk]], axis=0)
                vbt = jnp.concatenate([vp_ref[rows, :], vc_ref[rows, :blk]], axis=1)
            else:
                kb = kc_ref[kvh, (j - 1) * blk:(j + 1) * blk]
                vbt = vc_ref[rows, (j - 1) * blk:(j + 1) * blk]
            qcat = jnp.concatenate(
                [q_ref[(kvh * Q_PER_KV + gq) * HEAD_DIM:(kvh * Q_PER_KV + gq + 1) * HEAD_DIM, lanes]
                 for gq in range(Q_PER_KV)], axis=1)
            s2 = jnp.dot(kb, qcat, preferred_element_type=F32)
            s_prev = s2[:blk]
            if j == 0:
                s_prev = jnp.where(has_prev, s_prev, MASK_VALUE)
            s = jnp.where(from_prev, s_prev, s2[blk:])
            sink = sink_ref[kvh]
            m = jnp.maximum(jnp.max(s, axis=0, keepdims=True), sink)
            p = jnp.exp2(s - m).astype(BF16)
            p2 = jnp.concatenate([jnp.where(from_prev, p, zero), jnp.where(from_prev, zero, p)], axis=0)
            ov = jnp.dot(jnp.concatenate([vbt, ones_rows], axis=0), p2, preferred_element_type=F32)
            den = ov[HEAD_DIM:HEAD_DIM + 1] + jnp.exp2(sink - m)
            o = ov[:HEAD_DIM] * (1.0 / den)
            for gq in range(Q_PER_KV):
                r0 = (kvh * Q_PER_KV + gq) * HEAD_DIM
                o_ref[r0:r0 + HEAD_DIM, lanes] = o[:, gq * blk:(gq + 1) * blk].astype(BF16)
            if interleave is not None:
                interleave(j * n_kv + kvh, qb * n_kv)


def _qao_kernel(n_kv, qb, steps_per_seq, sink_ref, hn_ref, wq_ref, bq_ref, cos_ref, sin_ref, kc_ref, kp_ref,
                vc_ref, vp_ref, h_ref, wo_ref, bo_ref, g_ref, out_ref, q_even_ref, q_odd_ref, at_even_ref,
                at_odd_ref, mix_ref):
    s = pl.program_id(0)
    n_tiles = pl.num_programs(0) - 2
    attend_tile = jnp.clip(s - 1, 0, n_tiles - 1)

    @pl.when(s == 0)
    def _():
        q_odd_ref[...] = jnp.zeros_like(q_odd_ref)
        at_even_ref[...] = jnp.zeros_like(at_even_ref)

    def stage(q_new_ref, q_cur_ref, at_new_ref, at_done_ref):
        d = wo_ref.shape[1]
        tq = hn_ref.shape[0]
        a_done = at_done_ref[...].T

        def between(i, n):
            width = d // (n // 2)
            cols = slice((i // 2) * width, (i // 2 + 1) * width)
            if i % 2 == 1:
                mix_ref[:, cols] = jnp.dot(a_done, wo_ref[:, cols], preferred_element_type=F32) + bo_ref[:, cols]
            else:
                _project_q(hn_ref[...], wq_ref, bq_ref, cos_ref[...], sin_ref[...], q_new_ref, cols)

        _attend_tile(n_kv, qb, attend_tile % steps_per_seq, sink_ref, q_cur_ref, kc_ref, kp_ref, vc_ref, vp_ref,
                     at_new_ref, interleave=between)
        out_ref[...] = h_ref[...] + _rms_norm(mix_ref[...], g_ref[...])

    @pl.when(s % 2 == 0)
    def _():
        stage(q_even_ref, q_odd_ref, at_odd_ref, at_even_ref)

    @pl.when(s % 2 == 1)
    def _():
        stage(q_odd_ref, q_even_ref, at_even_ref, at_odd_ref)


def _attention_block(hn, k, vt, sinks, h, w_q, b_q, w_o, b_o, g_post, freq_tabs, bsz, seq, qb=4):
    t, d = h.shape
    n_kv = k.shape[0]
    blk = ATTN_BLOCK
    qb = min(qb, seq // blk)
    tq = qb * blk
    assert seq % tq == 0 and (qb * n_kv) % 2 == 0 and (2 * d // (qb * n_kv)) % HEAD_DIM == 0
    steps = seq // tq
    n_tiles = bsz * steps
    half = ROPE_DIM // 2
    sink_x = jnp.repeat(sinks.astype(F32) * LOG2_E, blk).reshape(n_kv, 1, Q_PER_KV * blk)
    proj = lambda s: jnp.minimum(s, n_tiles - 1)
    att = lambda s: jnp.clip(s - 1, 0, n_tiles - 1)
    prev = lambda s: (att(s) // steps) * steps * qb + jnp.maximum((att(s) % steps) * qb - 1, 0)
    done = lambda s: jnp.clip(s - 2, 0, n_tiles - 1)
    tab = pl.BlockSpec((half, tq), lambda s: (0, proj(s) % steps))
    return pl.pallas_call(
        functools.partial(_qao_kernel, n_kv, qb, steps),
        grid=(n_tiles + 2,),
        in_specs=[_resident((n_kv, 1, Q_PER_KV * blk)),
                  pl.BlockSpec((tq, d), lambda s: (proj(s), 0)), _resident((d, d)), _resident((1, d)), tab, tab,
                  pl.BlockSpec((n_kv, tq, HEAD_DIM), lambda s: (0, att(s), 0)),
                  pl.BlockSpec((n_kv, blk, HEAD_DIM), lambda s: (0, prev(s), 0)),
                  pl.BlockSpec((n_kv * HEAD_DIM, tq), lambda s: (0, att(s))),
                  pl.BlockSpec((n_kv * HEAD_DIM, blk), lambda s: (0, prev(s))),
                  pl.BlockSpec((tq, d), lambda s: (done(s), 0)),
                  _resident((d, d)), _resident((1, d)), _resident((1, d))],
        out_specs=pl.BlockSpec((tq, d), lambda s: (done(s), 0)),
        out_shape=jax.ShapeDtypeStruct((t, d), F32),
        scratch_shapes=[pltpu.VMEM((d, tq), BF16)] * 4 + [pltpu.VMEM((tq, d), F32)],
        compiler_params=_compiler_params(("arbitrary",)),
        name="swa_sink_attention_block",
    )(sink_x, hn, w_q.astype(BF16), b_q.reshape(1, d), *freq_tabs, k, k, vt, vt, h, w_o.astype(BF16),
      b_o.reshape(1, d), g_post.reshape(1, d))


def kernel(x, norm_g, ffn_w_gate, ffn_w_up, ffn_w_down, ssm_a_re, ssm_a_im, ssm_log_dt, ssm_b_re, ssm_b_im, ssm_c_re, ssm_c_im, ssm_d, glu_w_out, glu_w_gate, glu_b_gate, kv_norm_g, w_kv, b_kv, w_q, b_q, attn_sinks, w_o, b_o):
    bsz, seq, d = x.shape
    depth = norm_g.shape[0]
    n_a = ssm_a_re.shape[0]
    assert seq % ATTN_BLOCK == 0 and seq % SSM_CHUNK == 0 and d % LANES == 0
    h = x.astype(F32).reshape(bsz * seq, d)
    lane_tabs, freq_tabs = _rope_tables(seq)
    ffn_order = [(layer, which) for layer in range(depth) for which in range(2)]
    weights = tuple(w[0, 0].astype(BF16) for w in (ffn_w_gate, ffn_w_up, ffn_w_down))

    def following(layer, which):
        pos = ffn_order.index((layer, which)) + 1
        return (ffn_w_gate, ffn_w_up, ffn_w_down) + ffn_order[pos] if pos < len(ffn_order) else None

    k = vt = None
    for layer in range(depth):
        g = norm_g[layer].astype(F32)
        h, hn, weights = _ffn(h, g[0], g[1], weights, g_next=g[2], next_weights=following(layer, 0))
        if layer < n_a:
            ssm = _s5_params(ssm_a_re[layer], ssm_a_im[layer], ssm_log_dt[layer], ssm_b_re[layer].astype(F32),
                             ssm_b_im[layer].astype(F32), ssm_c_re[layer].astype(F32), ssm_c_im[layer].astype(F32))
            z = _s5_core(hn, bsz, seq, ssm, ssm_d[layer])
            h = _glu(z, h, glu_w_out[layer].astype(BF16), glu_w_gate[layer].astype(BF16), glu_b_gate[layer], g[3])
        else:
            bl = layer - n_a
            h = _attention_block(hn, k, vt, attn_sinks[bl], h, w_q[bl], b_q[bl], w_o[bl], b_o[bl], g[3], freq_tabs,
                                 bsz, seq)
        g_kv = kv_norm_g.astype(F32) if layer == n_a - 1 else None
        h, hn_kv, weights = _ffn(h, g[4], g[5], weights, g_next=g_kv, next_weights=following(layer, 1))
        if layer == n_a - 1:
            k, vt = _shared_kv(hn_kv, seq, w_kv, b_kv, lane_tabs)
    return h.reshape(bsz, seq, d).astype(x.dtype)
```

```python
import functools
import math

import jax
import jax.numpy as jnp
from jax import lax
from jax.experimental import pallas as pl
from jax.experimental.pallas import tpu as pltpu

F32 = jnp.float32
BF16 = jnp.bfloat16

NORM_EPS = 1e-6
FFN_RESIDUAL_WEIGHT = 0.5
SSM_GROUP = 16
SSM_CHUNK = 16
HEAD_DIM = 64
Q_PER_KV = 8
ATTN_BLOCK = 128
ROPE_DIM = HEAD_DIM // 4
ROPE_THETA = 500000.0
MASK_VALUE = -1e30
LOG2_E = math.log2(math.e)
LANES = 128
SUBLANES = 8
VMEM_LIMIT_BYTES = 63 * 1024 * 1024


def _compiler_params(semantics):
    return pltpu.CompilerParams(dimension_semantics=semantics, vmem_limit_bytes=VMEM_LIMIT_BYTES)


def _rms_norm(x, g):
    return x * lax.rsqrt(jnp.mean(x * x, axis=-1, keepdims=True) + NORM_EPS) * g


def _resident(shape):
    return pl.BlockSpec(shape, lambda *_: (0,) * len(shape), pipeline_mode=pl.Buffered(1))


def _row_tile(t, want):
    tm = min(t, want)
    assert t % tm == 0
    return tm


def _ffn_kernel(emit_next, cast_next, h_ref, gpre_ref, gpost_ref, *rest):
    rest = list(rest)
    gnext_ref = rest.pop(0) if emit_next else None
    wg_ref, wu_ref, wd_ref = rest[:3]
    rest = rest[3:]
    next_f32 = [rest.pop(0) for _ in range(3)] if cast_next else []
    o_ref = rest.pop(0)
    on_ref = rest.pop(0) if emit_next else None
    next_bf16 = [rest.pop(0) for _ in range(3)] if cast_next else []
    xn_ref, acc_ref = rest
    j = pl.program_id(1)

    for src, dst in zip(next_f32, next_bf16):
        dst[...] = src[...].astype(BF16)

    def swiglu_chunk():
        xn = xn_ref[...]
        gate = jnp.dot(xn, wg_ref[...], preferred_element_type=F32)
        up = jnp.dot(xn, wu_ref[...], preferred_element_type=F32)
        act = (gate * jax.nn.sigmoid(gate) * up).astype(BF16)
        return jnp.dot(act, wd_ref[...], preferred_element_type=F32)

    @pl.when(j == 0)
    def _():
        xn_ref[...] = _rms_norm(h_ref[...], gpre_ref[...]).astype(BF16)
        acc_ref[...] = swiglu_chunk()

    last = pl.num_programs(1) - 1

    @pl.when((j > 0) & (j < last))
    def _():
        acc_ref[...] += swiglu_chunk()

    @pl.when(j == last)
    def _():
        acc_ref[...] += swiglu_chunk()
        h_new = h_ref[...] + _rms_norm(acc_ref[...], gpost_ref[...])
        o_ref[...] = h_new
        if emit_next:
            on_ref[...] = _rms_norm(h_new, gnext_ref[...]).astype(BF16)


def _ffn(h, g_pre, g_post, weights, g_next=None, next_weights=None, tm=1024, tf=512):
    t, d = h.shape
    w_gate, w_up, w_down = weights
    f = w_gate.shape[-1]
    tm = _row_tile(t, tm)
    tf = _row_tile(f, tf)
    ni, nj = t // tm, f // tf
    emit_next = g_next is not None
    cast_next = next_weights is not None
    row = pl.BlockSpec((tm, d), lambda i, j: (i, 0))
    vec = pl.BlockSpec((1, d), lambda i, j: (0, 0))
    in_specs = [row, vec, vec] + ([vec] if emit_next else []) + [
        pl.BlockSpec((d, tf), lambda i, j: (0, j)),
        pl.BlockSpec((d, tf), lambda i, j: (0, j)),
        pl.BlockSpec((tf, d), lambda i, j: (j, 0)),
    ]
    args = [h, g_pre.reshape(1, d), (FFN_RESIDUAL_WEIGHT * g_post).reshape(1, d)] + (
        [g_next.reshape(1, d)] if emit_next else []) + [w_gate, w_up, w_down]
    out_shape = [jax.ShapeDtypeStruct((t, d), F32)] + ([jax.ShapeDtypeStruct((t, d), BF16)] if emit_next else [])
    out_row = pl.BlockSpec((tm, d), lambda i, j: (i, 0), pipeline_mode=pl.Buffered(1))
    out_specs = [out_row] + ([out_row] if emit_next else [])
    if cast_next:
        nwg, nwu, nwd, layer, which = next_weights
        dr = d // ni
        assert d % ni == 0 and dr % LANES == 0
        in_specs += [pl.BlockSpec((None, None, dr, tf), lambda i, j: (layer, which, i, j)),
                     pl.BlockSpec((None, None, dr, tf), lambda i, j: (layer, which, i, j)),
                     pl.BlockSpec((None, None, tf, dr), lambda i, j: (layer, which, j, i))]
        args += [nwg, nwu, nwd]
        out_specs += [pl.BlockSpec((dr, tf), lambda i, j: (i, j)), pl.BlockSpec((dr, tf), lambda i, j: (i, j)),
                      pl.BlockSpec((tf, dr), lambda i, j: (j, i))]
        out_shape += [jax.ShapeDtypeStruct((d, f), BF16), jax.ShapeDtypeStruct((d, f), BF16),
                      jax.ShapeDtypeStruct((f, d), BF16)]
    outs = pl.pallas_call(
        functools.partial(_ffn_kernel, emit_next, cast_next),
        grid=(ni, nj),
        in_specs=in_specs,
        out_specs=out_specs,
        out_shape=out_shape,
        scratch_shapes=[pltpu.VMEM((tm, d), BF16), pltpu.VMEM((tm, d), F32)],
        compiler_params=_compiler_params(("parallel", "arbitrary")),
        name="ffn",
    )(*args)
    h_new = outs[0]
    normed = outs[1] if emit_next else None
    casted = tuple(outs[-3:]) if cast_next else None
    return h_new, normed, casted


_NT = (((1,), (1,)), ((), ()))


def _s5_param_kernel(z0r_ref, z0i_ref, z1r_ref, z1i_ref, zvr_ref, zvi_ref, cr_ref, ci_ref, bbr_ref, bbi_ref,
                     kt_ref, bm_ref, cmt_ref):
    rows, p = z0r_ref.shape[1], z0r_ref.shape[2]

    def rep(a):
        return jnp.broadcast_to(a[:, None, :], (rows, SSM_GROUP, p)).reshape(rows * SSM_GROUP, p)

    def til(a):
        return jnp.broadcast_to(a[None, :, :], (rows, SSM_GROUP, p)).reshape(rows * SSM_GROUP, p)

    for gi in range(z0r_ref.shape[0]):
        cr, ci = til(cr_ref[gi]), til(ci_ref[gi])
        z0r, z0i = rep(z0r_ref[gi]), rep(z0i_ref[gi])
        zc_r = z0r * cr - z0i * ci
        zc_i = z0r * ci + z0i * cr
        kt_ref[gi] = (
            lax.dot_general(bbr_ref[gi], zc_r, _NT, precision=lax.Precision.HIGHEST, preferred_element_type=F32)
            - lax.dot_general(bbi_ref[gi], zc_i, _NT, precision=lax.Precision.HIGHEST, preferred_element_type=F32))
        z1r, z1i = rep(z1r_ref[gi]), rep(z1i_ref[gi])
        cmt_ref[gi] = jnp.concatenate([z1r * cr - z1i * ci, -(z1r * ci + z1i * cr)], axis=1).astype(BF16)
        zvr, zvi = rep(zvr_ref[gi]), rep(zvi_ref[gi])
        btr, bti = til(bbr_ref[gi]), til(bbi_ref[gi])
        bm_ref[gi] = jnp.concatenate([zvr * btr - zvi * bti, zvr * bti + zvi * btr], axis=1).astype(BF16)


def _s5_params(a_re, a_im, log_dt, b_re, b_im, c_re, c_im):
    g, p = a_re.shape
    gc, ch = SSM_GROUP, SSM_CHUNK
    n = ch * gc
    dt = jnp.exp(log_dt.astype(F32))[:, None]
    lam_re = a_re.astype(F32) * dt
    lam_im = a_im.astype(F32) * dt
    lags = jnp.arange(ch + 1, dtype=F32)[None, :, None]
    mag = jnp.exp(lags * lam_re[:, None, :])
    zr = mag * jnp.cos(lags * lam_im[:, None, :])
    zi = mag * jnp.sin(lags * lam_im[:, None, :])
    lb_re, lb_im = zr[:, 1], zi[:, 1]
    den = a_re * a_re + a_im * a_im
    num_re = lb_re - 1.0
    f_re = (num_re * a_re + lb_im * a_im) / den
    f_im = (lb_im * a_re - num_re * a_im) / den
    bbt_re = f_re[:, None, :] * jnp.swapaxes(b_re, 1, 2) - f_im[:, None, :] * jnp.swapaxes(b_im, 1, 2)
    bbt_im = f_re[:, None, :] * jnp.swapaxes(b_im, 1, 2) + f_im[:, None, :] * jnp.swapaxes(b_re, 1, 2)

    gp = min(g, GROUPS_PER_TILE)
    assert g % gp == 0
    blk = lambda *s: pl.BlockSpec((gp,) + s, lambda i: (i, 0, 0))
    kt, bmat, cmat_t = pl.pallas_call(
        _s5_param_kernel,
        grid=(g // gp,),
        in_specs=[blk(ch, p)] * 6 + [blk(gc, p)] * 4,
        out_specs=[blk(gc, n), blk(n, 2 * p), blk(n, 2 * p)],
        out_shape=[jax.ShapeDtypeStruct((g, gc, n), F32), jax.ShapeDtypeStruct((g, n, 2 * p), BF16),
                   jax.ShapeDtypeStruct((g, n, 2 * p), BF16)],
        compiler_params=_compiler_params(("parallel",)),
        name="s5_params",
    )(zr[:, :ch], zi[:, :ch], zr[:, 1:], zi[:, 1:], zr[:, ch - 1::-1][:, :ch], zi[:, ch - 1::-1][:, :ch],
      c_re, c_im, bbt_re, bbt_im)

    kt_pad = jnp.pad(kt, ((0, 0), (0, 0), (n, 0)))
    toep = jnp.stack([kt_pad[:, :, n - s * gc:2 * n - s * gc] for s in range(ch)], axis=1)
    toep = toep.reshape(g, n, n).astype(BF16)
    zl_r, zl_i = zr[:, ch], zi[:, ch]
    a1 = jnp.concatenate([zl_r, zl_r], axis=-1)
    a2 = jnp.concatenate([-zl_i, zl_i], axis=-1)
    a2s = jnp.concatenate([zl_i, -zl_i], axis=-1)
    return toep, bmat, cmat_t, a1, a2, a2s


GROUPS_PER_TILE = LANES // SSM_GROUP
TILES_PER_CHUNK_ROW = SSM_CHUNK * SSM_GROUP // LANES
RELAYOUT_ROWS = 16
RELAYOUT_UNROLL = 4
SCAN_GROUPS = 32
SCAN_UNROLL = 8


def _lane_block_ids(rows):
    return lax.broadcasted_iota(jnp.int32, (rows, LANES), 1) // SSM_GROUP


def _transpose_granules(v, blk_id):
    v = list(v)
    d = GROUPS_PER_TILE // 2
    while d >= 1:
        low = (blk_id & d) == 0
        nxt = list(v)
        for j in range(GROUPS_PER_TILE):
            if j & d:
                continue
            a, b = v[j], v[j + d]
            nxt[j] = jnp.where(low, a, pltpu.roll(b, d * SSM_GROUP, axis=1))
            nxt[j + d] = jnp.where(low, pltpu.roll(a, LANES - d * SSM_GROUP, axis=1), b)
        v = nxt
        d //= 2
    return v


def _s5_state_kernel(x_ref, bm_ref, u_ref, e_ref, xf_ref):
    ch, gc, rb = SSM_CHUNK, SSM_GROUP, RELAYOUT_ROWS
    n_chunks = u_ref.shape[1]
    xf_ref[...] = x_ref[...].astype(F32)
    blk_id = _lane_block_ids(rb)

    def body(i, carry):
        row0 = pl.multiple_of(i * rb, rb)
        tok = [xf_ref[pl.ds(row0 * ch + s, rb, stride=ch), :].astype(BF16) for s in range(ch)]
        for hf in range(TILES_PER_CHUNK_ROW):
            grp = _transpose_granules(tok[hf * GROUPS_PER_TILE:(hf + 1) * GROUPS_PER_TILE], blk_id)
            for g in range(GROUPS_PER_TILE):
                u_ref[g, pl.ds(row0, rb), hf * LANES:(hf + 1) * LANES] = grp[g]
        return carry

    lax.fori_loop(0, n_chunks // rb, body, 0, unroll=RELAYOUT_UNROLL)
    nc = e_ref.shape[1]
    for g in range(GROUPS_PER_TILE):
        e = jnp.dot(u_ref[g], bm_ref[g], preferred_element_type=F32)
        for b in range(e_ref.shape[0]):
            e_ref[b, :, g, :] = e[b * nc:(b + 1) * nc]


def _s5_scan_kernel(e_ref, a1_ref, a2_ref, a2s_ref, x0_ref):
    a1, a2, a2s = a1_ref[...], a2_ref[...], a2s_ref[...]
    half = a1.shape[-1] // 2

    def body(k, carry):
        s, sw = carry
        x0_ref[0, k] = s
        e = e_ref[0, k]
        e_sw = pltpu.roll(e, half, axis=1)
        return a1 * s + a2 * sw + e, a1 * sw + a2s * s + e_sw

    zero = jnp.zeros(a1.shape, F32)
    lax.fori_loop(0, e_ref.shape[1], body, (zero, zero), unroll=SCAN_UNROLL)


def _s5_out_kernel(u_ref, toep_ref, x0_ref, cmt_ref, d_ref, z_ref, y_ref, zf_ref):
    ch, rb = SSM_CHUNK, RELAYOUT_ROWS
    n_chunks = u_ref.shape[1]
    for g in range(GROUPS_PER_TILE):
        u = u_ref[g]
        x0 = jnp.concatenate([x0_ref[b, :, g, :] for b in range(x0_ref.shape[0])], axis=0).astype(BF16)
        y = (jnp.dot(u, toep_ref[g], preferred_element_type=F32)
             + lax.dot_general(x0, cmt_ref[g], _NT, preferred_element_type=F32)
             + d_ref[g] * u.astype(F32))
        y_ref[g] = jax.nn.gelu(y).astype(BF16)
    blk_id = _lane_block_ids(rb)

    def body(i, carry):
        row0 = pl.multiple_of(i * rb, rb)
        for hf in range(TILES_PER_CHUNK_ROW):
            grp = [y_ref[g, pl.ds(row0, rb), hf * LANES:(hf + 1) * LANES] for g in range(GROUPS_PER_TILE)]
            tok = _transpose_granules(grp, blk_id)
            for j in range(GROUPS_PER_TILE):
                s = hf * GROUPS_PER_TILE + j
                zf_ref[pl.ds(row0 * ch + s, rb, stride=ch), :] = tok[j].astype(F32)
        return carry

    lax.fori_loop(0, n_chunks // rb, body, 0, unroll=RELAYOUT_UNROLL)
    z_ref[...] = zf_ref[...].astype(BF16)


def _s5_core(hn, bsz, seq, ssm, d_skip):
    toep, bmat, cmat, a1, a2, a2s = ssm
    t, d = hn.shape
    g = d // SSM_GROUP
    gc, ch = SSM_GROUP, SSM_CHUNK
    n = gc * ch
    nc = seq // ch
    r = bsz * nc
    p2 = bmat.shape[-1]
    gb = GROUPS_PER_TILE
    assert r % RELAYOUT_ROWS == 0 and g % gb == 0

    u_g, e = pl.pallas_call(
        _s5_state_kernel,
        grid=(g // gb,),
        in_specs=[pl.BlockSpec((t, LANES), lambda i: (0, i)), pl.BlockSpec((gb, n, p2), lambda i: (i, 0, 0))],
        out_specs=[pl.BlockSpec((gb, r, n), lambda i: (i, 0, 0)),
                   pl.BlockSpec((bsz, nc, gb, p2), lambda i: (0, 0, i, 0))],
        out_shape=[jax.ShapeDtypeStruct((g, r, n), BF16), jax.ShapeDtypeStruct((bsz, nc, g, p2), F32)],
        scratch_shapes=[pltpu.VMEM((t, LANES), F32)],
        compiler_params=_compiler_params(("parallel",)),
        name="s5_chunk_state",
    )(hn, bmat)

    gs = min(g, SCAN_GROUPS)
    tab = pl.BlockSpec((gs, p2), lambda b, i: (i, 0))
    x0 = pl.pallas_call(
        _s5_scan_kernel,
        grid=(bsz, g // gs),
        in_specs=[pl.BlockSpec((1, nc, gs, p2), lambda b, i: (b, 0, i, 0)), tab, tab, tab],
        out_specs=pl.BlockSpec((1, nc, gs, p2), lambda b, i: (b, 0, i, 0)),
        out_shape=jax.ShapeDtypeStruct((bsz, nc, g, p2), F32),
        compiler_params=_compiler_params(("parallel", "parallel")),
        name="s5_chunk_scan",
    )(e, a1, a2, a2s)

    d_tile = jnp.tile(d_skip.astype(F32).reshape(g, 1, gc), (1, 1, ch))
    return pl.pallas_call(
        _s5_out_kernel,
        grid=(g // gb,),
        in_specs=[pl.BlockSpec((gb, r, n), lambda i: (i, 0, 0)), pl.BlockSpec((gb, n, n), lambda i: (i, 0, 0)),
                  pl.BlockSpec((bsz, nc, gb, p2), lambda i: (0, 0, i, 0)),
                  pl.BlockSpec((gb, n, p2), lambda i: (i, 0, 0)), pl.BlockSpec((gb, 1, n), lambda i: (i, 0, 0))],
        out_specs=pl.BlockSpec((t, LANES), lambda i: (0, i)),
        out_shape=jax.ShapeDtypeStruct((t, d), BF16),
        scratch_shapes=[pltpu.VMEM((gb, r, n), BF16), pltpu.VMEM((t, LANES), F32)],
        compiler_params=_compiler_params(("parallel",)),
        name="s5_chunk_out",
    )(u_g, toep, x0, cmat, d_tile)


def _glu_kernel(z_ref, h_ref, w1_ref, w2_ref, b_ref, g_ref, o_ref):
    z = z_ref[...]
    lin = jnp.dot(z, w1_ref[...], preferred_element_type=F32)
    gate = jnp.dot(z, w2_ref[...], preferred_element_type=F32) + b_ref[...]
    o_ref[...] = h_ref[...] + _rms_norm(lin * jax.nn.sigmoid(gate), g_ref[...])


def _glu(z, h, w_out, w_gate, b_gate, g_post, tm=512):
    t, d = h.shape
    tm = _row_tile(t, tm)
    row = pl.BlockSpec((tm, d), lambda i: (i, 0))
    return pl.pallas_call(
        _glu_kernel,
        grid=(t // tm,),
        in_specs=[row, row, _resident((d, d)), _resident((d, d)), _resident((1, d)), _resident((1, d))],
        out_specs=row,
        out_shape=jax.ShapeDtypeStruct((t, d), F32),
        compiler_params=_compiler_params(("parallel",)),
        name="s5_glu",
    )(z, h, w_out, w_gate, b_gate.reshape(1, d), g_post.reshape(1, d))


def _rope_tables(seq):
    half = ROPE_DIM // 2
    inv_freq = ROPE_THETA ** (-jnp.arange(half, dtype=F32) / half)
    ang = jnp.arange(seq, dtype=F32)[:, None] * inv_freq[None, :]
    cos, sin = jnp.cos(ang), jnp.sin(ang)
    ones = jnp.ones((seq, HEAD_DIM - ROPE_DIM), F32)
    zeros = jnp.zeros((seq, HEAD_DIM - half), F32)
    cos_h = jnp.concatenate([cos, cos, ones], axis=1)
    sin_up = jnp.concatenate([-sin, zeros], axis=1)
    sin_dn = jnp.concatenate([jnp.zeros((seq, half), F32), sin, jnp.zeros((seq, HEAD_DIM - ROPE_DIM), F32)], axis=1)
    rep = LANES // HEAD_DIM
    lane_tabs = (jnp.tile(cos_h, (1, rep)), jnp.tile(sin_up, (1, rep)), jnp.tile(sin_dn, (1, rep)))
    return lane_tabs, (cos.T, sin.T)


def _rope_lanes(x, cos, sin_up, sin_dn):
    half = ROPE_DIM // 2
    return (x * cos + pltpu.roll(x, LANES - half, axis=1) * sin_up + pltpu.roll(x, half, axis=1) * sin_dn)


def _kv_kernel(hn_ref, wk_ref, bk_ref, wvt_ref, bvt_ref, cos_ref, sup_ref, sdn_ref, k_ref, vt_ref):
    hn = hn_ref[...]
    k = jnp.dot(hn, wk_ref[...], preferred_element_type=F32) + bk_ref[...]
    cos, sup, sdn = cos_ref[...], sup_ref[...], sdn_ref[...]
    heads_per_tile = LANES // HEAD_DIM
    for c in range(k.shape[-1] // LANES):
        kr = _rope_lanes(k[:, c * LANES:(c + 1) * LANES], cos, sup, sdn).astype(BF16)
        for e in range(heads_per_tile):
            k_ref[c * heads_per_tile + e] = kr[:, e * HEAD_DIM:(e + 1) * HEAD_DIM]
    vt = lax.dot_general(wvt_ref[...], hn, _NT, preferred_element_type=F32) + bvt_ref[...]
    vt_ref[...] = vt.astype(BF16)


def _shared_kv(hn, seq, w_kv, b_kv, lane_tabs, tm=512):
    t, d = hn.shape
    kw = w_kv.shape[1] // 2
    n_kv = kw // HEAD_DIM
    assert kw % LANES == 0
    tm = _row_tile(seq, tm)
    row = pl.BlockSpec((tm, d), lambda i: (i, 0))
    tab = pl.BlockSpec((tm, LANES), lambda i: (i % (seq // tm), 0))
    wk = w_kv[:, :kw].astype(BF16)
    wvt = w_kv[:, kw:].T.astype(BF16)
    return pl.pallas_call(
        _kv_kernel,
        grid=(t // tm,),
        in_specs=[row, _resident((d, kw)), _resident((1, kw)), _resident((kw, d)), _resident((kw, 1)),
                  tab, tab, tab],
        out_specs=[pl.BlockSpec((n_kv, tm, HEAD_DIM), lambda i: (0, i, 0)), pl.BlockSpec((kw, tm), lambda i: (0, i))],
        out_shape=[jax.ShapeDtypeStruct((n_kv, t, HEAD_DIM), BF16), jax.ShapeDtypeStruct((kw, t), BF16)],
        compiler_params=_compiler_params(("parallel",)),
        name="shared_kv",
    )(hn, wk, b_kv[:kw].reshape(1, kw), wvt, b_kv[kw:].reshape(kw, 1), *lane_tabs)


def _q_kernel(hn_ref, wt_ref, bt_ref, cos_ref, sin_ref, q_ref):
    scale = HEAD_DIM ** -0.5 * LOG2_E
    qt = (lax.dot_general(wt_ref[...], hn_ref[...], _NT, preferred_element_type=F32) + bt_ref[...]) * scale
    q_ref[...] = qt.astype(BF16)
    cos, sin = cos_ref[...], sin_ref[...]
    half = ROPE_DIM // 2
    for head in range(qt.shape[0] // HEAD_DIM):
        r0 = head * HEAD_DIM
        t1, t2 = qt[r0:r0 + half], qt[r0 + half:r0 + ROPE_DIM]
        rot = jnp.concatenate([t1 * cos - t2 * sin, t2 * cos + t1 * sin], axis=0)
        q_ref[r0:r0 + ROPE_DIM, :] = rot.astype(BF16)


def _q_proj(hn, seq, w_q, b_q, freq_tabs, tm=512):
    t, d = hn.shape
    tm = _row_tile(seq, tm)
    half = ROPE_DIM // 2
    tab = pl.BlockSpec((half, tm), lambda i: (0, i % (seq // tm)))
    return pl.pallas_call(
        _q_kernel,
        grid=(t // tm,),
        in_specs=[pl.BlockSpec((tm, d), lambda i: (i, 0)), _resident((d, d)), _resident((d, 1)), tab, tab],
        out_specs=pl.BlockSpec((d, tm), lambda i: (0, i)),
        out_shape=jax.ShapeDtypeStruct((d, t), BF16),
        compiler_params=_compiler_params(("parallel",)),
        name="q_proj",
    )(hn, w_q.T.astype(BF16), b_q.reshape(d, 1), *freq_tabs)


def _attend_tile(n_kv, qb, step_in_seq, sink_ref, q_ref, kc_ref, kp_ref, vc_ref, vp_ref, o_ref, interleave=None):
    blk = ATTN_BLOCK
    width = Q_PER_KV * blk
    r = lax.broadcasted_iota(jnp.int32, (blk, width), 0)
    qi = lax.broadcasted_iota(jnp.int32, (blk, width), 1) % blk
    from_prev = r > qi
    has_prev = (step_in_seq > 0) | (r < 0)
    zero = jnp.zeros((blk, width), BF16)
    ones_rows = jnp.ones((SUBLANES, 2 * blk), BF16)
    for j in range(qb):
        lanes = slice(j * blk, (j + 1) * blk)
        for kvh in range(n_kv):
            rows = slice(kvh * HEAD_DIM, (kvh + 1) * HEAD_DIM)
            if j == 0:
                kb = jnp.concatenate([kp_ref[kvh], kc_ref[kvh, :blk]], axis=0)
                vbt = jnp.concatenate([vp_ref[rows, :], vc_ref[rows, :blk]], axis=1)
            else:
                kb = kc_ref[kvh, (j - 1) * blk:(j + 1) * blk]
                vbt = vc_ref[rows, (j - 1) * blk:(j + 1) * blk]
            qcat = jnp.concatenate(
                [q_ref[(kvh * Q_PER_KV + gq) * HEAD_DIM:(kvh * Q_PER_KV + gq + 1) * HEAD_DIM, lanes]
                 for gq in range(Q_PER_KV)], axis=1)
            s2 = jnp.dot(kb, qcat, preferred_element_type=F32)
            s_prev = s2[:blk]
            if j == 0:
                s_prev = jnp.where(has_prev, s_prev, MASK_VALUE)
            s = jnp.where(from_prev, s_prev, s2[blk:])
            sink = sink_ref[kvh]
            m = jnp.maximum(jnp.max(s, axis=0, keepdims=True), sink)
            p = jnp.exp2(s - m).astype(BF16)
            p2 = jnp.concatenate([jnp.where(from_prev, p, zero), jnp.where(from_prev, zero, p)], axis=0)
            ov = jnp.dot(jnp.concatenate([vbt, ones_rows], axis=0), p2, preferred_element_type=F32)
            den = ov[HEAD_DIM:HEAD_DIM + 1] + jnp.exp2(sink - m)
            o = ov[:HEAD_DIM] * (1.0 / den)
            for gq in range(Q_PER_KV):
                r0 = (kvh * Q_PER_KV + gq) * HEAD_DIM
                o_ref[r0:r0 + HEAD_DIM, lanes] = o[:, gq * blk:(gq + 1) * blk].astype(BF16)
            if interleave is not None:
                interleave(j * n_kv + kvh, qb * n_kv)


def _attn_o_kernel(n_kv, qb, steps_per_seq, sink_ref, q_ref, kc_ref, kp_ref, vc_ref, vp_ref, h_ref, w_ref, b_ref,
                   g_ref, out_ref, at_even_ref, at_odd_ref, mix_ref):
    s = pl.program_id(0)
    tile = jnp.minimum(s, pl.num_programs(0) - 2)

    @pl.when(s == 0)
    def _():
        at_odd_ref[...] = jnp.zeros_like(at_odd_ref)

    def stage(done_ref, next_ref):
        d = w_ref.shape[1]
        a_prev = done_ref[...].T

        def project_chunk(i, n):
            if i % 2 == 0:
                return
            width = d // (n // 2)
            cols = slice((i // 2) * width, (i // 2 + 1) * width)
            mix_ref[:, cols] = jnp.dot(a_prev, w_ref[:, cols], preferred_element_type=F32) + b_ref[:, cols]

        _attend_tile(n_kv, qb, tile % steps_per_seq, sink_ref, q_ref, kc_ref, kp_ref, vc_ref, vp_ref, next_ref,
                     interleave=project_chunk)
        out_ref[...] = h_ref[...] + _rms_norm(mix_ref[...], g_ref[...])

    @pl.when(s % 2 == 0)
    def _():
        stage(at_odd_ref, at_even_ref)

    @pl.when(s % 2 == 1)
    def _():
        stage(at_even_ref, at_odd_ref)


def _attention_o_proj(qt, k, vt, sinks, h, w_o, b_o, g_post, bsz, seq, qb=4):
    d, t = qt.shape
    n_kv = k.shape[0]
    blk = ATTN_BLOCK
    qb = min(qb, seq // blk)
    assert seq % (qb * blk) == 0
    tq = qb * blk
    steps = seq // tq
    n_tiles = bsz * steps
    sink_x = jnp.repeat(sinks.astype(F32) * LOG2_E, blk).reshape(n_kv, 1, Q_PER_KV * blk)
    cur = lambda s: jnp.minimum(s, n_tiles - 1)
    prev = lambda s: (cur(s) // steps) * steps * qb + jnp.maximum((cur(s) % steps) * qb - 1, 0)
    done = lambda s: jnp.maximum(s - 1, 0)
    return pl.pallas_call(
        functools.partial(_attn_o_kernel, n_kv, qb, steps),
        grid=(n_tiles + 1,),
        in_specs=[_resident((n_kv, 1, Q_PER_KV * blk)),
                  pl.BlockSpec((d, tq), lambda s: (0, cur(s))),
                  pl.BlockSpec((n_kv, tq, HEAD_DIM), lambda s: (0, cur(s), 0)),
                  pl.BlockSpec((n_kv, blk, HEAD_DIM), lambda s: (0, prev(s), 0)),
                  pl.BlockSpec((n_kv * HEAD_DIM, tq), lambda s: (0, cur(s))),
                  pl.BlockSpec((n_kv * HEAD_DIM, blk), lambda s: (0, prev(s))),
                  pl.BlockSpec((tq, d), lambda s: (done(s), 0)),
                  _resident((d, d)), _resident((1, d)), _resident((1, d))],
        out_specs=pl.BlockSpec((tq, d), lambda s: (done(s), 0)),
        out_shape=jax.ShapeDtypeStruct((t, d), F32),
        scratch_shapes=[pltpu.VMEM((d, tq), BF16), pltpu.VMEM((d, tq), BF16), pltpu.VMEM((tq, d), F32)],
        compiler_params=_compiler_params(("arbitrary",)),
        name="swa_sink_attention_o_proj",
    )(sink_x, qt, k, k, vt, vt, h, w_o.astype(BF16), b_o.reshape(1, d), g_post.reshape(1, d))


def kernel(x, norm_g, ffn_w_gate, ffn_w_up, ffn_w_down, ssm_a_re, ssm_a_im, ssm_log_dt, ssm_b_re, ssm_b_im, ssm_c_re, ssm_c_im, ssm_d, glu_w_out, glu_w_gate, glu_b_gate, kv_norm_g, w_kv, b_kv, w_q, b_q, attn_sinks, w_o, b_o):
    bsz, seq, d = x.shape
    depth = norm_g.shape[0]
    n_a = ssm_a_re.shape[0]
    assert seq % ATTN_BLOCK == 0 and seq % SSM_CHUNK == 0 and d % LANES == 0
    h = x.astype(F32).reshape(bsz * seq, d)
    lane_tabs, freq_tabs = _rope_tables(seq)
    ffn_order = [(layer, which) for layer in range(depth) for which in range(2)]
    weights = tuple(w[0, 0].astype(BF16) for w in (ffn_w_gate, ffn_w_up, ffn_w_down))

    def following(layer, which):
        pos = ffn_order.index((layer, which)) + 1
        return (ffn_w_gate, ffn_w_up, ffn_w_down) + ffn_order[pos] if pos < len(ffn_order) else None

    k = vt = None
    for layer in range(depth):
        g = norm_g[layer].astype(F32)
        h, hn, weights = _ffn(h, g[0], g[1], weights, g_next=g[2], next_weights=following(layer, 0))
        if layer < n_a:
            ssm = _s5_params(ssm_a_re[layer], ssm_a_im[layer], ssm_log_dt[layer], ssm_b_re[layer].astype(F32),
                             ssm_b_im[layer].astype(F32), ssm_c_re[layer].astype(F32), ssm_c_im[layer].astype(F32))
            z = _s5_core(hn, bsz, seq, ssm, ssm_d[layer])
            h = _glu(z, h, glu_w_out[layer].astype(BF16), glu_w_gate[layer].astype(BF16), glu_b_gate[layer], g[3])
        else:
            bl = layer - n_a
            qt = _q_proj(hn, seq, w_q[bl], b_q[bl], freq_tabs)
            h = _attention_o_proj(qt, k, vt, attn_sinks[bl], h, w_o[bl], b_o[bl], g[3], bsz, seq)
        g_kv = kv_norm_g.astype(F32) if layer == n_a - 1 else None
        h, hn_kv, weights = _ffn(h, g[4], g[5], weights, g_next=g_kv, next_weights=following(layer, 1))
        if layer == n_a - 1:
            k, vt = _shared_kv(hn_kv, seq, w_kv, b_kv, lane_tabs)
    return h.reshape(bsz, seq, d).astype(x.dtype)
```

```python
import functools
import math

import jax
import jax.numpy as jnp
from jax import lax
from jax.experimental import pallas as pl
from jax.experimental.pallas import tpu as pltpu

F32 = jnp.float32
BF16 = jnp.bfloat16

NORM_EPS = 1e-6
FFN_RESIDUAL_WEIGHT = 0.5
SSM_GROUP = 16
SSM_CHUNK = 16
HEAD_DIM = 64
Q_PER_KV = 8
ATTN_BLOCK = 128
ROPE_DIM = HEAD_DIM // 4
ROPE_THETA = 500000.0
MASK_VALUE = -1e30
LOG2_E = math.log2(math.e)
LANES = 128
SUBLANES = 8
VMEM_LIMIT_BYTES = 63 * 1024 * 1024


def _compiler_params(semantics):
    return pltpu.CompilerParams(dimension_semantics=semantics, vmem_limit_bytes=VMEM_LIMIT_BYTES)


def _rms_norm(x, g):
    return x * lax.rsqrt(jnp.mean(x * x, axis=-1, keepdims=True) + NORM_EPS) * g


def _resident(shape):
    return pl.BlockSpec(shape, lambda *_: (0,) * len(shape), pipeline_mode=pl.Buffered(1))


def _row_tile(t, want):
    tm = min(t, want)
    assert t % tm == 0
    return tm


def _ffn_kernel(emit_next, cast_next, h_ref, gpre_ref, gpost_ref, *rest):
    rest = list(rest)
    gnext_ref = rest.pop(0) if emit_next else None
    wg_ref, wu_ref, wd_ref = rest[:3]
    rest = rest[3:]
    next_f32 = [rest.pop(0) for _ in range(3)] if cast_next else []
    o_ref = rest.pop(0)
    on_ref = rest.pop(0) if emit_next else None
    next_bf16 = [rest.pop(0) for _ in range(3)] if cast_next else []
    xn_ref, acc_ref = rest
    j = pl.program_id(1)

    for src, dst in zip(next_f32, next_bf16):
        dst[...] = src[...].astype(BF16)

    def swiglu_chunk():
        xn = xn_ref[...]
        gate = jnp.dot(xn, wg_ref[...], preferred_element_type=F32)
        up = jnp.dot(xn, wu_ref[...], preferred_element_type=F32)
        act = (gate * jax.nn.sigmoid(gate) * up).astype(BF16)
        return jnp.dot(act, wd_ref[...], preferred_element_type=F32)

    @pl.when(j == 0)
    def _():
        xn_ref[...] = _rms_norm(h_ref[...], gpre_ref[...]).astype(BF16)
        acc_ref[...] = swiglu_chunk()

    last = pl.num_programs(1) - 1

    @pl.when((j > 0) & (j < last))
    def _():
        acc_ref[...] += swiglu_chunk()

    @pl.when(j == last)
    def _():
        acc_ref[...] += swiglu_chunk()
        h_new = h_ref[...] + _rms_norm(acc_ref[...], gpost_ref[...])
        o_ref[...] = h_new
        if emit_next:
            on_ref[...] = _rms_norm(h_new, gnext_ref[...]).astype(BF16)


def _ffn(h, g_pre, g_post, weights, g_next=None, next_weights=None, tm=1024, tf=512):
    t, d = h.shape
    w_gate, w_up, w_down = weights
    f = w_gate.shape[-1]
    tm = _row_tile(t, tm)
    tf = _row_tile(f, tf)
    ni, nj = t // tm, f // tf
    emit_next = g_next is not None
    cast_next = next_weights is not None
    row = pl.BlockSpec((tm, d), lambda i, j: (i, 0))
    vec = pl.BlockSpec((1, d), lambda i, j: (0, 0))
    in_specs = [row, vec, vec] + ([vec] if emit_next else []) + [
        pl.BlockSpec((d, tf), lambda i, j: (0, j)),
        pl.BlockSpec((d, tf), lambda i, j: (0, j)),
        pl.BlockSpec((tf, d), lambda i, j: (j, 0)),
    ]
    args = [h, g_pre.reshape(1, d), (FFN_RESIDUAL_WEIGHT * g_post).reshape(1, d)] + (
        [g_next.reshape(1, d)] if emit_next else []) + [w_gate, w_up, w_down]
    out_shape = [jax.ShapeDtypeStruct((t, d), F32)] + ([jax.ShapeDtypeStruct((t, d), BF16)] if emit_next else [])
    out_row = pl.BlockSpec((tm, d), lambda i, j: (i, 0), pipeline_mode=pl.Buffered(1))
    out_specs = [out_row] + ([out_row] if emit_next else [])
    if cast_next:
        nwg, nwu, nwd, layer, which = next_weights
        dr = d // ni
        assert d % ni == 0 and dr % LANES == 0
        in_specs += [pl.BlockSpec((None, None, dr, tf), lambda i, j: (layer, which, i, j)),
                     pl.BlockSpec((None, None, dr, tf), lambda i, j: (layer, which, i, j)),
                     pl.BlockSpec((None, None, tf, dr), lambda i, j: (layer, which, j, i))]
        args += [nwg, nwu, nwd]
        out_specs += [pl.BlockSpec((dr, tf), lambda i, j: (i, j)), pl.BlockSpec((dr, tf), lambda i, j: (i, j)),
                      pl.BlockSpec((tf, dr), lambda i, j: (j, i))]
        out_shape += [jax.ShapeDtypeStruct((d, f), BF16), jax.ShapeDtypeStruct((d, f), BF16),
                      jax.ShapeDtypeStruct((f, d), BF16)]
    outs = pl.pallas_call(
        functools.partial(_ffn_kernel, emit_next, cast_next),
        grid=(ni, nj),
        in_specs=in_specs,
        out_specs=out_specs,
        out_shape=out_shape,
        scratch_shapes=[pltpu.VMEM((tm, d), BF16), pltpu.VMEM((tm, d), F32)],
        compiler_params=_compiler_params(("parallel", "arbitrary")),
        name="ffn",
    )(*args)
    h_new = outs[0]
    normed = outs[1] if emit_next else None
    casted = tuple(outs[-3:]) if cast_next else None
    return h_new, normed, casted


_NT = (((1,), (1,)), ((), ()))


def _s5_param_kernel(z0r_ref, z0i_ref, z1r_ref, z1i_ref, zvr_ref, zvi_ref, cr_ref, ci_ref, bbr_ref, bbi_ref,
                     kt_ref, bm_ref, cmt_ref):
    rows, p = z0r_ref.shape[1], z0r_ref.shape[2]

    def rep(a):
        return jnp.broadcast_to(a[:, None, :], (rows, SSM_GROUP, p)).reshape(rows * SSM_GROUP, p)

    def til(a):
        return jnp.broadcast_to(a[None, :, :], (rows, SSM_GROUP, p)).reshape(rows * SSM_GROUP, p)

    for gi in range(z0r_ref.shape[0]):
        cr, ci = til(cr_ref[gi]), til(ci_ref[gi])
        z0r, z0i = rep(z0r_ref[gi]), rep(z0i_ref[gi])
        zc_r = z0r * cr - z0i * ci
        zc_i = z0r * ci + z0i * cr
        kt_ref[gi] = (
            lax.dot_general(bbr_ref[gi], zc_r, _NT, precision=lax.Precision.HIGHEST, preferred_element_type=F32)
            - lax.dot_general(bbi_ref[gi], zc_i, _NT, precision=lax.Precision.HIGHEST, preferred_element_type=F32))
        z1r, z1i = rep(z1r_ref[gi]), rep(z1i_ref[gi])
        cmt_ref[gi] = jnp.concatenate([z1r * cr - z1i * ci, -(z1r * ci + z1i * cr)], axis=1).astype(BF16)
        zvr, zvi = rep(zvr_ref[gi]), rep(zvi_ref[gi])
        btr, bti = til(bbr_ref[gi]), til(bbi_ref[gi])
        bm_ref[gi] = jnp.concatenate([zvr * btr - zvi * bti, zvr * bti + zvi * btr], axis=1).astype(BF16)


def _s5_params(a_re, a_im, log_dt, b_re, b_im, c_re, c_im):
    g, p = a_re.shape
    gc, ch = SSM_GROUP, SSM_CHUNK
    n = ch * gc
    dt = jnp.exp(log_dt.astype(F32))[:, None]
    lam_re = a_re.astype(F32) * dt
    lam_im = a_im.astype(F32) * dt
    lags = jnp.arange(ch + 1, dtype=F32)[None, :, None]
    mag = jnp.exp(lags * lam_re[:, None, :])
    zr = mag * jnp.cos(lags * lam_im[:, None, :])
    zi = mag * jnp.sin(lags * lam_im[:, None, :])
    lb_re, lb_im = zr[:, 1], zi[:, 1]
    den = a_re * a_re + a_im * a_im
    num_re = lb_re - 1.0
    f_re = (num_re * a_re + lb_im * a_im) / den
    f_im = (lb_im * a_re - num_re * a_im) / den
    bbt_re = f_re[:, None, :] * jnp.swapaxes(b_re, 1, 2) - f_im[:, None, :] * jnp.swapaxes(b_im, 1, 2)
    bbt_im = f_re[:, None, :] * jnp.swapaxes(b_im, 1, 2) + f_im[:, None, :] * jnp.swapaxes(b_re, 1, 2)

    gp = min(g, GROUPS_PER_TILE)
    assert g % gp == 0
    blk = lambda *s: pl.BlockSpec((gp,) + s, lambda i: (i, 0, 0))
    kt, bmat, cmat_t = pl.pallas_call(
        _s5_param_kernel,
        grid=(g // gp,),
        in_specs=[blk(ch, p)] * 6 + [blk(gc, p)] * 4,
        out_specs=[blk(gc, n), blk(n, 2 * p), blk(n, 2 * p)],
        out_shape=[jax.ShapeDtypeStruct((g, gc, n), F32), jax.ShapeDtypeStruct((g, n, 2 * p), BF16),
                   jax.ShapeDtypeStruct((g, n, 2 * p), BF16)],
        compiler_params=_compiler_params(("parallel",)),
        name="s5_params",
    )(zr[:, :ch], zi[:, :ch], zr[:, 1:], zi[:, 1:], zr[:, ch - 1::-1][:, :ch], zi[:, ch - 1::-1][:, :ch],
      c_re, c_im, bbt_re, bbt_im)

    kt_pad = jnp.pad(kt, ((0, 0), (0, 0), (n, 0)))
    toep = jnp.stack([kt_pad[:, :, n - s * gc:2 * n - s * gc] for s in range(ch)], axis=1)
    toep = toep.reshape(g, n, n).astype(BF16)
    zl_r, zl_i = zr[:, ch], zi[:, ch]
    a1 = jnp.concatenate([zl_r, zl_r], axis=-1)
    a2 = jnp.concatenate([-zl_i, zl_i], axis=-1)
    a2s = jnp.concatenate([zl_i, -zl_i], axis=-1)
    return toep, bmat, cmat_t, a1, a2, a2s


GROUPS_PER_TILE = LANES // SSM_GROUP
TILES_PER_CHUNK_ROW = SSM_CHUNK * SSM_GROUP // LANES
RELAYOUT_ROWS = 16
RELAYOUT_UNROLL = 4
SCAN_GROUPS = 32
SCAN_UNROLL = 8


def _lane_block_ids(rows):
    return lax.broadcasted_iota(jnp.int32, (rows, LANES), 1) // SSM_GROUP


def _transpose_granules(v, blk_id):
    n = GROUPS_PER_TILE
    rot = [pltpu.roll(x, j * SSM_GROUP, axis=1) if j else x for j, x in enumerate(v)]
    out = []
    for i in range(n):
        acc = rot[(-i) % n]
        for p in range(1, n):
            acc = jnp.where(blk_id == p, rot[(p - i) % n], acc)
        out.append(pltpu.roll(acc, LANES - i * SSM_GROUP, axis=1) if i else acc)
    return out


def _s5_state_kernel(x_ref, bm_ref, u_ref, e_ref, xf_ref):
    ch, gc, rb = SSM_CHUNK, SSM_GROUP, RELAYOUT_ROWS
    n_chunks = u_ref.shape[1]
    xf_ref[...] = x_ref[...].astype(F32)
    blk_id = _lane_block_ids(rb)

    def body(i, carry):
        row0 = pl.multiple_of(i * rb, rb)
        tok = [xf_ref[pl.ds(row0 * ch + s, rb, stride=ch), :].astype(BF16) for s in range(ch)]
        for hf in range(TILES_PER_CHUNK_ROW):
            grp = _transpose_granules(tok[hf * GROUPS_PER_TILE:(hf + 1) * GROUPS_PER_TILE], blk_id)
            for g in range(GROUPS_PER_TILE):
                u_ref[g, pl.ds(row0, rb), hf * LANES:(hf + 1) * LANES] = grp[g]
        return carry

    lax.fori_loop(0, n_chunks // rb, body, 0, unroll=RELAYOUT_UNROLL)
    nc = e_ref.shape[1]
    for g in range(GROUPS_PER_TILE):
        e = jnp.dot(u_ref[g], bm_ref[g], preferred_element_type=F32)
        for b in range(e_ref.shape[0]):
            e_ref[b, :, g, :] = e[b * nc:(b + 1) * nc]


def _s5_scan_kernel(e_ref, a1_ref, a2_ref, a2s_ref, x0_ref):
    a1, a2, a2s = a1_ref[...], a2_ref[...], a2s_ref[...]
    half = a1.shape[-1] // 2

    def body(k, carry):
        s, sw = carry
        x0_ref[0, k] = s
        e = e_ref[0, k]
        e_sw = pltpu.roll(e, half, axis=1)
        return a1 * s + a2 * sw + e, a1 * sw + a2s * s + e_sw

    zero = jnp.zeros(a1.shape, F32)
    lax.fori_loop(0, e_ref.shape[1], body, (zero, zero), unroll=SCAN_UNROLL)


def _s5_out_kernel(u_ref, toep_ref, x0_ref, cmt_ref, d_ref, z_ref, y_ref, zf_ref):
    ch, rb = SSM_CHUNK, RELAYOUT_ROWS
    n_chunks = u_ref.shape[1]
    for g in range(GROUPS_PER_TILE):
        u = u_ref[g]
        x0 = jnp.concatenate([x0_ref[b, :, g, :] for b in range(x0_ref.shape[0])], axis=0).astype(BF16)
        y = (jnp.dot(u, toep_ref[g], preferred_element_type=F32)
             + lax.dot_general(x0, cmt_ref[g], _NT, preferred_element_type=F32)
             + d_ref[g] * u.astype(F32))
        y_ref[g] = jax.nn.gelu(y).astype(BF16)
    blk_id = _lane_block_ids(rb)

    def body(i, carry):
        row0 = pl.multiple_of(i * rb, rb)
        for hf in range(TILES_PER_CHUNK_ROW):
            grp = [y_ref[g, pl.ds(row0, rb), hf * LANES:(hf + 1) * LANES] for g in range(GROUPS_PER_TILE)]
            tok = _transpose_granules(grp, blk_id)
            for j in range(GROUPS_PER_TILE):
                s = hf * GROUPS_PER_TILE + j
                zf_ref[pl.ds(row0 * ch + s, rb, stride=ch), :] = tok[j].astype(F32)
        return carry

    lax.fori_loop(0, n_chunks // rb, body, 0, unroll=RELAYOUT_UNROLL)
    z_ref[...] = zf_ref[...].astype(BF16)


def _s5_core(hn, bsz, seq, ssm, d_skip):
    toep, bmat, cmat, a1, a2, a2s = ssm
    t, d = hn.shape
    g = d // SSM_GROUP
    gc, ch = SSM_GROUP, SSM_CHUNK
    n = gc * ch
    nc = seq // ch
    r = bsz * nc
    p2 = bmat.shape[-1]
    gb = GROUPS_PER_TILE
    assert r % RELAYOUT_ROWS == 0 and g % gb == 0

    u_g, e = pl.pallas_call(
        _s5_state_kernel,
        grid=(g // gb,),
        in_specs=[pl.BlockSpec((t, LANES), lambda i: (0, i)), pl.BlockSpec((gb, n, p2), lambda i: (i, 0, 0))],
        out_specs=[pl.BlockSpec((gb, r, n), lambda i: (i, 0, 0)),
                   pl.BlockSpec((bsz, nc, gb, p2), lambda i: (0, 0, i, 0))],
        out_shape=[jax.ShapeDtypeStruct((g, r, n), BF16), jax.ShapeDtypeStruct((bsz, nc, g, p2), F32)],
        scratch_shapes=[pltpu.VMEM((t, LANES), F32)],
        compiler_params=_compiler_params(("parallel",)),
        name="s5_chunk_state",
    )(hn, bmat)

    gs = min(g, SCAN_GROUPS)
    tab = pl.BlockSpec((gs, p2), lambda b, i: (i, 0))
    x0 = pl.pallas_call(
        _s5_scan_kernel,
        grid=(bsz, g // gs),
        in_specs=[pl.BlockSpec((1, nc, gs, p2), lambda b, i: (b, 0, i, 0)), tab, tab, tab],
        out_specs=pl.BlockSpec((1, nc, gs, p2), lambda b, i: (b, 0, i, 0)),
        out_shape=jax.ShapeDtypeStruct((bsz, nc, g, p2), F32),
        compiler_params=_compiler_params(("parallel", "parallel")),
        name="s5_chunk_scan",
    )(e, a1, a2, a2s)

    d_tile = jnp.tile(d_skip.astype(F32).reshape(g, 1, gc), (1, 1, ch))
    return pl.pallas_call(
        _s5_out_kernel,
        grid=(g // gb,),
        in_specs=[pl.BlockSpec((gb, r, n), lambda i: (i, 0, 0)), pl.BlockSpec((gb, n, n), lambda i: (i, 0, 0)),
                  pl.BlockSpec((bsz, nc, gb, p2), lambda i: (0, 0, i, 0)),
                  pl.BlockSpec((gb, n, p2), lambda i: (i, 0, 0)), pl.BlockSpec((gb, 1, n), lambda i: (i, 0, 0))],
        out_specs=pl.BlockSpec((t, LANES), lambda i: (0, i)),
        out_shape=jax.ShapeDtypeStruct((t, d), BF16),
        scratch_shapes=[pltpu.VMEM((gb, r, n), BF16), pltpu.VMEM((t, LANES), F32)],
        compiler_params=_compiler_params(("parallel",)),
        name="s5_chunk_out",
    )(u_g, toep, x0, cmat, d_tile)


def _glu_kernel(z_ref, h_ref, w1_ref, w2_ref, b_ref, g_ref, o_ref):
    z = z_ref[...]
    lin = jnp.dot(z, w1_ref[...], preferred_element_type=F32)
    gate = jnp.dot(z, w2_ref[...], preferred_element_type=F32) + b_ref[...]
    o_ref[...] = h_ref[...] + _rms_norm(lin * jax.nn.sigmoid(gate), g_ref[...])


def _glu(z, h, w_out, w_gate, b_gate, g_post, tm=512):
    t, d = h.shape
    tm = _row_tile(t, tm)
    row = pl.BlockSpec((tm, d), lambda i: (i, 0))
    return pl.pallas_call(
        _glu_kernel,
        grid=(t // tm,),
        in_specs=[row, row, _resident((d, d)), _resident((d, d)), _resident((1, d)), _resident((1, d))],
        out_specs=row,
        out_shape=jax.ShapeDtypeStruct((t, d), F32),
        compiler_params=_compiler_params(("parallel",)),
        name="s5_glu",
    )(z, h, w_out, w_gate, b_gate.reshape(1, d), g_post.reshape(1, d))


def _rope_tables(seq):
    half = ROPE_DIM // 2
    inv_freq = ROPE_THETA ** (-jnp.arange(half, dtype=F32) / half)
    ang = jnp.arange(seq, dtype=F32)[:, None] * inv_freq[None, :]
    cos, sin = jnp.cos(ang), jnp.sin(ang)
    ones = jnp.ones((seq, HEAD_DIM - ROPE_DIM), F32)
    zeros = jnp.zeros((seq, HEAD_DIM - half), F32)
    cos_h = jnp.concatenate([cos, cos, ones], axis=1)
    sin_up = jnp.concatenate([-sin, zeros], axis=1)
    sin_dn = jnp.concatenate([jnp.zeros((seq, half), F32), sin, jnp.zeros((seq, HEAD_DIM - ROPE_DIM), F32)], axis=1)
    rep = LANES // HEAD_DIM
    lane_tabs = (jnp.tile(cos_h, (1, rep)), jnp.tile(sin_up, (1, rep)), jnp.tile(sin_dn, (1, rep)))
    return lane_tabs, (cos.T, sin.T)


def _rope_lanes(x, cos, sin_up, sin_dn):
    half = ROPE_DIM // 2
    return (x * cos + pltpu.roll(x, LANES - half, axis=1) * sin_up + pltpu.roll(x, half, axis=1) * sin_dn)


def _kv_kernel(hn_ref, wk_ref, bk_ref, wvt_ref, bvt_ref, cos_ref, sup_ref, sdn_ref, k_ref, vt_ref):
    hn = hn_ref[...]
    k = jnp.dot(hn, wk_ref[...], preferred_element_type=F32) + bk_ref[...]
    cos, sup, sdn = cos_ref[...], sup_ref[...], sdn_ref[...]
    heads_per_tile = LANES // HEAD_DIM
    for c in range(k.shape[-1] // LANES):
        kr = _rope_lanes(k[:, c * LANES:(c + 1) * LANES], cos, sup, sdn).astype(BF16)
        for e in range(heads_per_tile):
            k_ref[c * heads_per_tile + e] = kr[:, e * HEAD_DIM:(e + 1) * HEAD_DIM]
    vt = lax.dot_general(wvt_ref[...], hn, _NT, preferred_element_type=F32) + bvt_ref[...]
    vt_ref[...] = vt.astype(BF16)


def _shared_kv(hn, seq, w_kv, b_kv, lane_tabs, tm=512):
    t, d = hn.shape
    kw = w_kv.shape[1] // 2
    n_kv = kw // HEAD_DIM
    assert kw % LANES == 0
    tm = _row_tile(seq, tm)
    row = pl.BlockSpec((tm, d), lambda i: (i, 0))
    tab = pl.BlockSpec((tm, LANES), lambda i: (i % (seq // tm), 0))
    wk = w_kv[:, :kw].astype(BF16)
    wvt = w_kv[:, kw:].T.astype(BF16)
    return pl.pallas_call(
        _kv_kernel,
        grid=(t // tm,),
        in_specs=[row, _resident((d, kw)), _resident((1, kw)), _resident((kw, d)), _resident((kw, 1)),
                  tab, tab, tab],
        out_specs=[pl.BlockSpec((n_kv, tm, HEAD_DIM), lambda i: (0, i, 0)), pl.BlockSpec((kw, tm), lambda i: (0, i))],
        out_shape=[jax.ShapeDtypeStruct((n_kv, t, HEAD_DIM), BF16), jax.ShapeDtypeStruct((kw, t), BF16)],
        compiler_params=_compiler_params(("parallel",)),
        name="shared_kv",
    )(hn, wk, b_kv[:kw].reshape(1, kw), wvt, b_kv[kw:].reshape(kw, 1), *lane_tabs)


def _q_kernel(hn_ref, wt_ref, bt_ref, cos_ref, sin_ref, q_ref):
    scale = HEAD_DIM ** -0.5 * LOG2_E
    qt = (lax.dot_general(wt_ref[...], hn_ref[...], _NT, preferred_element_type=F32) + bt_ref[...]) * scale
    q_ref[...] = qt.astype(BF16)
    cos, sin = cos_ref[...], sin_ref[...]
    half = ROPE_DIM // 2
    for head in range(qt.shape[0] // HEAD_DIM):
        r0 = head * HEAD_DIM
        t1, t2 = qt[r0:r0 + half], qt[r0 + half:r0 + ROPE_DIM]
        rot = jnp.concatenate([t1 * cos - t2 * sin, t2 * cos + t1 * sin], axis=0)
        q_ref[r0:r0 + ROPE_DIM, :] = rot.astype(BF16)


def _q_proj(hn, seq, w_q, b_q, freq_tabs, tm=512):
    t, d = hn.shape
    tm = _row_tile(seq, tm)
    half = ROPE_DIM // 2
    tab = pl.BlockSpec((half, tm), lambda i: (0, i % (seq // tm)))
    return pl.pallas_call(
        _q_kernel,
        grid=(t // tm,),
        in_specs=[pl.BlockSpec((tm, d), lambda i: (i, 0)), _resident((d, d)), _resident((d, 1)), tab, tab],
        out_specs=pl.BlockSpec((d, tm), lambda i: (0, i)),
        out_shape=jax.ShapeDtypeStruct((d, t), BF16),
        compiler_params=_compiler_params(("parallel",)),
        name="q_proj",
    )(hn, w_q.T.astype(BF16), b_q.reshape(d, 1), *freq_tabs)


def _attend_tile(n_kv, qb, step_in_seq, sink_ref, q_ref, kc_ref, kp_ref, vc_ref, vp_ref, o_ref, interleave=None):
    blk = ATTN_BLOCK
    width = Q_PER_KV * blk
    r = lax.broadcasted_iota(jnp.int32, (blk, width), 0)
    qi = lax.broadcasted_iota(jnp.int32, (blk, width), 1) % blk
    from_prev = r > qi
    has_prev = (step_in_seq > 0) | (r < 0)
    zero = jnp.zeros((blk, width), BF16)
    ones_rows = jnp.ones((SUBLANES, 2 * blk), BF16)
    for j in range(qb):
        lanes = slice(j * blk, (j + 1) * blk)
        for kvh in range(n_kv):
            rows = slice(kvh * HEAD_DIM, (kvh + 1) * HEAD_DIM)
            if j == 0:
                kb = jnp.concatenate([kp_ref[kvh], kc_ref[kvh, :blk]], axis=0)
                vbt = jnp.concatenate([vp_ref[rows, :], vc_ref[rows, :blk]], axis=1)
            else:
                kb = kc_ref[kvh, (j - 1) * blk:(j + 1) * blk]
                vbt = vc_ref[rows, (j - 1) * blk:(j + 1) * blk]
            qcat = jnp.concatenate(
                [q_ref[(kvh * Q_PER_KV + gq) * HEAD_DIM:(kvh * Q_PER_KV + gq + 1) * HEAD_DIM, lanes]
                 for gq in range(Q_PER_KV)], axis=1)
            s2 = jnp.dot(kb, qcat, preferred_element_type=F32)
            s_prev = s2[:blk]
            if j == 0:
                s_prev = jnp.where(has_prev, s_prev, MASK_VALUE)
            s = jnp.where(from_prev, s_prev, s2[blk:])
            sink = sink_ref[kvh]
            m = jnp.maximum(jnp.max(s, axis=0, keepdims=True), sink)
            p = jnp.exp2(s - m).astype(BF16)
            p2 = jnp.concatenate([jnp.where(from_prev, p, zero), jnp.where(from_prev, zero, p)], axis=0)
            ov = jnp.dot(jnp.concatenate([vbt, ones_rows], axis=0), p2, preferred_element_type=F32)
            den = ov[HEAD_DIM:HEAD_DIM + 1] + jnp.exp2(sink - m)
            o = ov[:HEAD_DIM] * (1.0 / den)
            for gq in range(Q_PER_KV):
                r0 = (kvh * Q_PER_KV + gq) * HEAD_DIM
                o_ref[r0:r0 + HEAD_DIM, lanes] = o[:, gq * blk:(gq + 1) * blk].astype(BF16)
            if interleave is not None:
                interleave(j * n_kv + kvh, qb * n_kv)


def _attn_o_kernel(n_kv, qb, steps_per_seq, sink_ref, q_ref, kc_ref, kp_ref, vc_ref, vp_ref, h_ref, w_ref, b_ref,
                   g_ref, out_ref, at_even_ref, at_odd_ref, mix_ref):
    s = pl.program_id(0)
    tile = jnp.minimum(s, pl.num_programs(0) - 2)

    @pl.when(s == 0)
    def _():
        at_odd_ref[...] = jnp.zeros_like(at_odd_ref)

    def stage(done_ref, next_ref):
        d = w_ref.shape[1]
        a_prev = done_ref[...].T

        def project_chunk(i, n):
            if i % 2 == 0:
                return
            width = d // (n // 2)
            cols = slice((i // 2) * width, (i // 2 + 1) * width)
            mix_ref[:, cols] = jnp.dot(a_prev, w_ref[:, cols], preferred_element_type=F32) + b_ref[:, cols]

        _attend_tile(n_kv, qb, tile % steps_per_seq, sink_ref, q_ref, kc_ref, kp_ref, vc_ref, vp_ref, next_ref,
                     interleave=project_chunk)
        out_ref[...] = h_ref[...] + _rms_norm(mix_ref[...], g_ref[...])

    @pl.when(s % 2 == 0)
    def _():
        stage(at_odd_ref, at_even_ref)

    @pl.when(s % 2 == 1)
    def _():
        stage(at_even_ref, at_odd_ref)


def _attention_o_proj(qt, k, vt, sinks, h, w_o, b_o, g_post, bsz, seq, qb=4):
    d, t = qt.shape
    n_kv = k.shape[0]
    blk = ATTN_BLOCK
    qb = min(qb, seq // blk)
    assert seq % (qb * blk) == 0
    tq = qb * blk
    steps = seq // tq
    n_tiles = bsz * steps
    sink_x = jnp.repeat(sinks.astype(F32) * LOG2_E, blk).reshape(n_kv, 1, Q_PER_KV * blk)
    cur = lambda s: jnp.minimum(s, n_tiles - 1)
    prev = lambda s: (cur(s) // steps) * steps * qb + jnp.maximum((cur(s) % steps) * qb - 1, 0)
    done = lambda s: jnp.maximum(s - 1, 0)
    return pl.pallas_call(
        functools.partial(_attn_o_kernel, n_kv, qb, steps),
        grid=(n_tiles + 1,),
        in_specs=[_resident((n_kv, 1, Q_PER_KV * blk)),
                  pl.BlockSpec((d, tq), lambda s: (0, cur(s))),
                  pl.BlockSpec((n_kv, tq, HEAD_DIM), lambda s: (0, cur(s), 0)),
                  pl.BlockSpec((n_kv, blk, HEAD_DIM), lambda s: (0, prev(s), 0)),
                  pl.BlockSpec((n_kv * HEAD_DIM, tq), lambda s: (0, cur(s))),
                  pl.BlockSpec((n_kv * HEAD_DIM, blk), lambda s: (0, prev(s))),
                  pl.BlockSpec((tq, d), lambda s: (done(s), 0)),
                  _resident((d, d)), _resident((1, d)), _resident((1, d))],
        out_specs=pl.BlockSpec((tq, d), lambda s: (done(s), 0)),
        out_shape=jax.ShapeDtypeStruct((t, d), F32),
        scratch_shapes=[pltpu.VMEM((d, tq), BF16), pltpu.VMEM((d, tq), BF16), pltpu.VMEM((tq, d), F32)],
        compiler_params=_compiler_params(("arbitrary",)),
        name="swa_sink_attention_o_proj",
    )(sink_x, qt, k, k, vt, vt, h, w_o.astype(BF16), b_o.reshape(1, d), g_post.reshape(1, d))


def kernel(x, norm_g, ffn_w_gate, ffn_w_up, ffn_w_down, ssm_a_re, ssm_a_im, ssm_log_dt, ssm_b_re, ssm_b_im, ssm_c_re, ssm_c_im, ssm_d, glu_w_out, glu_w_gate, glu_b_gate, kv_norm_g, w_kv, b_kv, w_q, b_q, attn_sinks, w_o, b_o):
    bsz, seq, d = x.shape
    depth = norm_g.shape[0]
    n_a = ssm_a_re.shape[0]
    assert seq % ATTN_BLOCK == 0 and seq % SSM_CHUNK == 0 and d % LANES == 0
    h = x.astype(F32).reshape(bsz * seq, d)
    lane_tabs, freq_tabs = _rope_tables(seq)
    ffn_order = [(layer, which) for layer in range(depth) for which in range(2)]
    weights = tuple(w[0, 0].astype(BF16) for w in (ffn_w_gate, ffn_w_up, ffn_w_down))

    def following(layer, which):
        pos = ffn_order.index((layer, which)) + 1
        return (ffn_w_gate, ffn_w_up, ffn_w_down) + ffn_order[pos] if pos < len(ffn_order) else None

    k = vt = None
    for layer in range(depth):
        g = norm_g[layer].astype(F32)
        h, hn, weights = _ffn(h, g[0], g[1], weights, g_next=g[2], next_weights=following(layer, 0))
        if layer < n_a:
            ssm = _s5_params(ssm_a_re[layer], ssm_a_im[layer], ssm_log_dt[layer], ssm_b_re[layer].astype(F32),
                             ssm_b_im[layer].astype(F32), ssm_c_re[layer].astype(F32), ssm_c_im[layer].astype(F32))
            z = _s5_core(hn, bsz, seq, ssm, ssm_d[layer])
            h = _glu(z, h, glu_w_out[layer].astype(BF16), glu_w_gate[layer].astype(BF16), glu_b_gate[layer], g[3])
        else:
            bl = layer - n_a
            qt = _q_proj(hn, seq, w_q[bl], b_q[bl], freq_tabs)
            h = _attention_o_proj(qt, k, vt, attn_sinks[bl], h, w_o[bl], b_o[bl], g[3], bsz, seq)
        g_kv = kv_norm_g.astype(F32) if layer == n_a - 1 else None
        h, hn_kv, weights = _ffn(h, g[4], g[5], weights, g_next=g_kv, next_weights=following(layer, 1))
        if layer == n_a - 1:
            k, vt = _shared_kv(hn_kv, seq, w_kv, b_kv, lane_tabs)
    return h.reshape(bsz, seq, d).astype(x.dtype)
```

```python
import functools
import math

import jax
import jax.numpy as jnp
from jax import lax
from jax.experimental import pallas as pl
from jax.experimental.pallas import tpu as pltpu

F32 = jnp.float32
BF16 = jnp.bfloat16

NORM_EPS = 1e-6
FFN_RESIDUAL_WEIGHT = 0.5
SSM_GROUP = 16
SSM_CHUNK = 16
HEAD_DIM = 64
Q_PER_KV = 8
ATTN_BLOCK = 128
ROPE_DIM = HEAD_DIM // 4
ROPE_THETA = 500000.0
MASK_VALUE = -1e30
LOG2_E = math.log2(math.e)
LANES = 128
SUBLANES = 8
VMEM_LIMIT_BYTES = 63 * 1024 * 1024


def _compiler_params(semantics):
    return pltpu.CompilerParams(dimension_semantics=semantics, vmem_limit_bytes=VMEM_LIMIT_BYTES)


def _rms_norm(x, g):
    return x * lax.rsqrt(jnp.mean(x * x, axis=-1, keepdims=True) + NORM_EPS) * g


def _resident(shape):
    return pl.BlockSpec(shape, lambda *_: (0,) * len(shape), pipeline_mode=pl.Buffered(1))


def _row_tile(t, want):
    tm = min(t, want)
    assert t % tm == 0
    return tm


def _ffn_kernel(emit_next, cast_next, h_ref, gpre_ref, gpost_ref, *rest):
    rest = list(rest)
    gnext_ref = rest.pop(0) if emit_next else None
    wg_ref, wu_ref, wd_ref = rest[:3]
    rest = rest[3:]
    next_f32 = [rest.pop(0) for _ in range(3)] if cast_next else []
    o_ref = rest.pop(0)
    on_ref = rest.pop(0) if emit_next else None
    next_bf16 = [rest.pop(0) for _ in range(3)] if cast_next else []
    xn_ref, acc_ref = rest
    j = pl.program_id(1)

    for src, dst in zip(next_f32, next_bf16):
        dst[...] = src[...].astype(BF16)

    def swiglu_chunk():
        xn = xn_ref[...]
        gate = jnp.dot(xn, wg_ref[...], preferred_element_type=F32)
        up = jnp.dot(xn, wu_ref[...], preferred_element_type=F32)
        act = (gate * jax.nn.sigmoid(gate) * up).astype(BF16)
        return jnp.dot(act, wd_ref[...], preferred_element_type=F32)

    @pl.when(j == 0)
    def _():
        xn_ref[...] = _rms_norm(h_ref[...], gpre_ref[...]).astype(BF16)
        acc_ref[...] = swiglu_chunk()

    last = pl.num_programs(1) - 1

    @pl.when((j > 0) & (j < last))
    def _():
        acc_ref[...] += swiglu_chunk()

    @pl.when(j == last)
    def _():
        acc_ref[...] += swiglu_chunk()
        h_new = h_ref[...] + _rms_norm(acc_ref[...], gpost_ref[...])
        o_ref[...] = h_new
        if emit_next:
            on_ref[...] = _rms_norm(h_new, gnext_ref[...]).astype(BF16)


def _ffn(h, g_pre, g_post, weights, g_next=None, next_weights=None, tm=1024, tf=512):
    t, d = h.shape
    w_gate, w_up, w_down = weights
    f = w_gate.shape[-1]
    tm = _row_tile(t, tm)
    tf = _row_tile(f, tf)
    ni, nj = t // tm, f // tf
    emit_next = g_next is not None
    cast_next = next_weights is not None
    row = pl.BlockSpec((tm, d), lambda i, j: (i, 0))
    vec = pl.BlockSpec((1, d), lambda i, j: (0, 0))
    in_specs = [row, vec, vec] + ([vec] if emit_next else []) + [
        pl.BlockSpec((d, tf), lambda i, j: (0, j)),
        pl.BlockSpec((d, tf), lambda i, j: (0, j)),
        pl.BlockSpec((tf, d), lambda i, j: (j, 0)),
    ]
    args = [h, g_pre.reshape(1, d), (FFN_RESIDUAL_WEIGHT * g_post).reshape(1, d)] + (
        [g_next.reshape(1, d)] if emit_next else []) + [w_gate, w_up, w_down]
    out_shape = [jax.ShapeDtypeStruct((t, d), F32)] + ([jax.ShapeDtypeStruct((t, d), BF16)] if emit_next else [])
    out_row = pl.BlockSpec((tm, d), lambda i, j: (i, 0), pipeline_mode=pl.Buffered(1))
    out_specs = [out_row] + ([out_row] if emit_next else [])
    if cast_next:
        nwg, nwu, nwd, layer, which = next_weights
        dr = d // ni
        assert d % ni == 0 and dr % LANES == 0
        in_specs += [pl.BlockSpec((None, None, dr, tf), lambda i, j: (layer, which, i, j)),
                     pl.BlockSpec((None, None, dr, tf), lambda i, j: (layer, which, i, j)),
                     pl.BlockSpec((None, None, tf, dr), lambda i, j: (layer, which, j, i))]
        args += [nwg, nwu, nwd]
        out_specs += [pl.BlockSpec((dr, tf), lambda i, j: (i, j)), pl.BlockSpec((dr, tf), lambda i, j: (i, j)),
                      pl.BlockSpec((tf, dr), lambda i, j: (j, i))]
        out_shape += [jax.ShapeDtypeStruct((d, f), BF16), jax.ShapeDtypeStruct((d, f), BF16),
                      jax.ShapeDtypeStruct((f, d), BF16)]
    outs = pl.pallas_call(
        functools.partial(_ffn_kernel, emit_next, cast_next),
        grid=(ni, nj),
        in_specs=in_specs,
        out_specs=out_specs,
        out_shape=out_shape,
        scratch_shapes=[pltpu.VMEM((tm, d), BF16), pltpu.VMEM((tm, d), F32)],
        compiler_params=_compiler_params(("parallel", "arbitrary")),
        name="ffn",
    )(*args)
    h_new = outs[0]
    normed = outs[1] if emit_next else None
    casted = tuple(outs[-3:]) if cast_next else None
    return h_new, normed, casted


_NT = (((1,), (1,)), ((), ()))


def _s5_param_kernel(z0r_ref, z0i_ref, z1r_ref, z1i_ref, zvr_ref, zvi_ref, cr_ref, ci_ref, bbr_ref, bbi_ref,
                     kt_ref, bm_ref, cmt_ref):
    rows, p = z0r_ref.shape[1], z0r_ref.shape[2]

    def rep(a):
        return jnp.broadcast_to(a[:, None, :], (rows, SSM_GROUP, p)).reshape(rows * SSM_GROUP, p)

    def til(a):
        return jnp.broadcast_to(a[None, :, :], (rows, SSM_GROUP, p)).reshape(rows * SSM_GROUP, p)

    for gi in range(z0r_ref.shape[0]):
        cr, ci = til(cr_ref[gi]), til(ci_ref[gi])
        z0r, z0i = rep(z0r_ref[gi]), rep(z0i_ref[gi])
        zc_r = z0r * cr - z0i * ci
        zc_i = z0r * ci + z0i * cr
        kt_ref[gi] = (
            lax.dot_general(bbr_ref[gi], zc_r, _NT, precision=lax.Precision.HIGHEST, preferred_element_type=F32)
            - lax.dot_general(bbi_ref[gi], zc_i, _NT, precision=lax.Precision.HIGHEST, preferred_element_type=F32))
        z1r, z1i = rep(z1r_ref[gi]), rep(z1i_ref[gi])
        cmt_ref[gi] = jnp.concatenate([z1r * cr - z1i * ci, -(z1r * ci + z1i * cr)], axis=1).astype(BF16)
        zvr, zvi = rep(zvr_ref[gi]), rep(zvi_ref[gi])
        btr, bti = til(bbr_ref[gi]), til(bbi_ref[gi])
        bm_ref[gi] = jnp.concatenate([zvr * btr - zvi * bti, zvr * bti + zvi * btr], axis=1).astype(BF16)


def _s5_params(a_re, a_im, log_dt, b_re, b_im, c_re, c_im):
    g, p = a_re.shape
    gc, ch = SSM_GROUP, SSM_CHUNK
    n = ch * gc
    dt = jnp.exp(log_dt.astype(F32))[:, None]
    lam_re = a_re.astype(F32) * dt
    lam_im = a_im.astype(F32) * dt
    lags = jnp.arange(ch + 1, dtype=F32)[None, :, None]
    mag = jnp.exp(lags * lam_re[:, None, :])
    zr = mag * jnp.cos(lags * lam_im[:, None, :])
    zi = mag * jnp.sin(lags * lam_im[:, None, :])
    lb_re, lb_im = zr[:, 1], zi[:, 1]
    den = a_re * a_re + a_im * a_im
    num_re = lb_re - 1.0
    f_re = (num_re * a_re + lb_im * a_im) / den
    f_im = (lb_im * a_re - num_re * a_im) / den
    bbt_re = f_re[:, None, :] * jnp.swapaxes(b_re, 1, 2) - f_im[:, None, :] * jnp.swapaxes(b_im, 1, 2)
    bbt_im = f_re[:, None, :] * jnp.swapaxes(b_im, 1, 2) + f_im[:, None, :] * jnp.swapaxes(b_re, 1, 2)

    gp = min(g, GROUPS_PER_TILE)
    assert g % gp == 0
    blk = lambda *s: pl.BlockSpec((gp,) + s, lambda i: (i, 0, 0))
    kt, bmat, cmat_t = pl.pallas_call(
        _s5_param_kernel,
        grid=(g // gp,),
        in_specs=[blk(ch, p)] * 6 + [blk(gc, p)] * 4,
        out_specs=[blk(gc, n), blk(n, 2 * p), blk(n, 2 * p)],
        out_shape=[jax.ShapeDtypeStruct((g, gc, n), F32), jax.ShapeDtypeStruct((g, n, 2 * p), BF16),
                   jax.ShapeDtypeStruct((g, n, 2 * p), BF16)],
        compiler_params=_compiler_params(("parallel",)),
        name="s5_params",
    )(zr[:, :ch], zi[:, :ch], zr[:, 1:], zi[:, 1:], zr[:, ch - 1::-1][:, :ch], zi[:, ch - 1::-1][:, :ch],
      c_re, c_im, bbt_re, bbt_im)

    kt_pad = jnp.pad(kt, ((0, 0), (0, 0), (n, 0)))
    toep = jnp.stack([kt_pad[:, :, n - s * gc:2 * n - s * gc] for s in range(ch)], axis=1)
    toep = toep.reshape(g, n, n).astype(BF16)
    zl_r, zl_i = zr[:, ch], zi[:, ch]
    a1 = jnp.concatenate([zl_r, zl_r], axis=-1)
    a2 = jnp.concatenate([-zl_i, zl_i], axis=-1)
    a2s = jnp.concatenate([zl_i, -zl_i], axis=-1)
    return toep, bmat, cmat_t, a1, a2, a2s


GROUPS_PER_TILE = LANES // SSM_GROUP
TILES_PER_CHUNK_ROW = SSM_CHUNK * SSM_GROUP // LANES
RELAYOUT_ROWS = 16
RELAYOUT_UNROLL = 4
SCAN_GROUPS = 32
SCAN_UNROLL = 8


def _lane_block_ids(rows):
    return lax.broadcasted_iota(jnp.int32, (rows, LANES), 1) // SSM_GROUP


def _transpose_granules(v, blk_id):
    n = GROUPS_PER_TILE
    rot = [pltpu.roll(x, j * SSM_GROUP, axis=1) if j else x for j, x in enumerate(v)]
    out = []
    for i in range(n):
        acc = rot[(-i) % n]
        for p in range(1, n):
            acc = jnp.where(blk_id == p, rot[(p - i) % n], acc)
        out.append(pltpu.roll(acc, LANES - i * SSM_GROUP, axis=1) if i else acc)
    return out


def _s5_state_kernel(x_ref, bm_ref, u_ref, e_ref, xf_ref):
    ch, gc, rb = SSM_CHUNK, SSM_GROUP, RELAYOUT_ROWS
    n_chunks = u_ref.shape[1]
    xf_ref[...] = x_ref[...].astype(F32)
    blk_id = _lane_block_ids(rb)

    def body(i, carry):
        row0 = pl.multiple_of(i * rb, rb)
        tok = [xf_ref[pl.ds(row0 * ch + s, rb, stride=ch), :].astype(BF16) for s in range(ch)]
        for hf in range(TILES_PER_CHUNK_ROW):
            grp = _transpose_granules(tok[hf * GROUPS_PER_TILE:(hf + 1) * GROUPS_PER_TILE], blk_id)
            for g in range(GROUPS_PER_TILE):
                u_ref[g, pl.ds(row0, rb), hf * LANES:(hf + 1) * LANES] = grp[g]
        return carry

    lax.fori_loop(0, n_chunks // rb, body, 0, unroll=RELAYOUT_UNROLL)
    nc = e_ref.shape[1]
    for g in range(GROUPS_PER_TILE):
        e = jnp.dot(u_ref[g], bm_ref[g], preferred_element_type=F32)
        for b in range(e_ref.shape[0]):
            e_ref[b, :, g, :] = e[b * nc:(b + 1) * nc]


def _s5_scan_kernel(e_ref, a1_ref, a2_ref, a2s_ref, x0_ref):
    a1, a2, a2s = a1_ref[...], a2_ref[...], a2s_ref[...]
    half = a1.shape[-1] // 2

    def body(k, carry):
        s, sw = carry
        x0_ref[0, k] = s
        e = e_ref[0, k]
        e_sw = pltpu.roll(e, half, axis=1)
        return a1 * s + a2 * sw + e, a1 * sw + a2s * s + e_sw

    zero = jnp.zeros(a1.shape, F32)
    lax.fori_loop(0, e_ref.shape[1], body, (zero, zero), unroll=SCAN_UNROLL)


def _s5_out_kernel(u_ref, toep_ref, x0_ref, cmt_ref, d_ref, z_ref, y_ref, zf_ref):
    ch, rb = SSM_CHUNK, RELAYOUT_ROWS
    n_chunks = u_ref.shape[1]
    for g in range(GROUPS_PER_TILE):
        u = u_ref[g]
        x0 = jnp.concatenate([x0_ref[b, :, g, :] for b in range(x0_ref.shape[0])], axis=0).astype(BF16)
        y = (jnp.dot(u, toep_ref[g], preferred_element_type=F32)
             + lax.dot_general(x0, cmt_ref[g], _NT, preferred_element_type=F32)
             + d_ref[g] * u.astype(F32))
        y_ref[g] = jax.nn.gelu(y).astype(BF16)
    blk_id = _lane_block_ids(rb)

    def body(i, carry):
        row0 = pl.multiple_of(i * rb, rb)
        for hf in range(TILES_PER_CHUNK_ROW):
            grp = [y_ref[g, pl.ds(row0, rb), hf * LANES:(hf + 1) * LANES] for g in range(GROUPS_PER_TILE)]
            tok = _transpose_granules(grp, blk_id)
            for j in range(GROUPS_PER_TILE):
                s = hf * GROUPS_PER_TILE + j
                zf_ref[pl.ds(row0 * ch + s, rb, stride=ch), :] = tok[j].astype(F32)
        return carry

    lax.fori_loop(0, n_chunks // rb, body, 0, unroll=RELAYOUT_UNROLL)
    z_ref[...] = zf_ref[...].astype(BF16)


def _s5_core(hn, bsz, seq, ssm, d_skip):
    toep, bmat, cmat, a1, a2, a2s = ssm
    t, d = hn.shape
    g = d // SSM_GROUP
    gc, ch = SSM_GROUP, SSM_CHUNK
    n = gc * ch
    nc = seq // ch
    r = bsz * nc
    p2 = bmat.shape[-1]
    gb = GROUPS_PER_TILE
    assert r % RELAYOUT_ROWS == 0 and g % gb == 0

    u_g, e = pl.pallas_call(
        _s5_state_kernel,
        grid=(g // gb,),
        in_specs=[pl.BlockSpec((t, LANES), lambda i: (0, i)), pl.BlockSpec((gb, n, p2), lambda i: (i, 0, 0))],
        out_specs=[pl.BlockSpec((gb, r, n), lambda i: (i, 0, 0)),
                   pl.BlockSpec((bsz, nc, gb, p2), lambda i: (0, 0, i, 0))],
        out_shape=[jax.ShapeDtypeStruct((g, r, n), BF16), jax.ShapeDtypeStruct((bsz, nc, g, p2), F32)],
        scratch_shapes=[pltpu.VMEM((t, LANES), F32)],
        compiler_params=_compiler_params(("parallel",)),
        name="s5_chunk_state",
    )(hn, bmat)

    gs = min(g, SCAN_GROUPS)
    tab = pl.BlockSpec((gs, p2), lambda b, i: (i, 0))
    x0 = pl.pallas_call(
        _s5_scan_kernel,
        grid=(bsz, g // gs),
        in_specs=[pl.BlockSpec((1, nc, gs, p2), lambda b, i: (b, 0, i, 0)), tab, tab, tab],
        out_specs=pl.BlockSpec((1, nc, gs, p2), lambda b, i: (b, 0, i, 0)),
        out_shape=jax.ShapeDtypeStruct((bsz, nc, g, p2), F32),
        compiler_params=_compiler_params(("parallel", "parallel")),
        name="s5_chunk_scan",
    )(e, a1, a2, a2s)

    d_tile = jnp.tile(d_skip.astype(F32).reshape(g, 1, gc), (1, 1, ch))
    return pl.pallas_call(
        _s5_out_kernel,
        grid=(g // gb,),
        in_specs=[pl.BlockSpec((gb, r, n), lambda i: (i, 0, 0)), pl.BlockSpec((gb, n, n), lambda i: (i, 0, 0)),
                  pl.BlockSpec((bsz, nc, gb, p2), lambda i: (0, 0, i, 0)),
                  pl.BlockSpec((gb, n, p2), lambda i: (i, 0, 0)), pl.BlockSpec((gb, 1, n), lambda i: (i, 0, 0))],
        out_specs=pl.BlockSpec((t, LANES), lambda i: (0, i)),
        out_shape=jax.ShapeDtypeStruct((t, d), BF16),
        scratch_shapes=[pltpu.VMEM((gb, r, n), BF16), pltpu.VMEM((t, LANES), F32)],
        compiler_params=_compiler_params(("parallel",)),
        name="s5_chunk_out",
    )(u_g, toep, x0, cmat, d_tile)


def _glu_kernel(z_ref, h_ref, w1_ref, w2_ref, b_ref, g_ref, o_ref):
    z = z_ref[...]
    lin = jnp.dot(z, w1_ref[...], preferred_element_type=F32)
    gate = jnp.dot(z, w2_ref[...], preferred_element_type=F32) + b_ref[...]
    o_ref[...] = h_ref[...] + _rms_norm(lin * jax.nn.sigmoid(gate), g_ref[...])


def _glu(z, h, w_out, w_gate, b_gate, g_post, tm=512):
    t, d = h.shape
    tm = _row_tile(t, tm)
    row = pl.BlockSpec((tm, d), lambda i: (i, 0))
    return pl.pallas_call(
        _glu_kernel,
        grid=(t // tm,),
        in_specs=[row, row, _resident((d, d)), _resident((d, d)), _resident((1, d)), _resident((1, d))],
        out_specs=row,
        out_shape=jax.ShapeDtypeStruct((t, d), F32),
        compiler_params=_compiler_params(("parallel",)),
        name="s5_glu",
    )(z, h, w_out, w_gate, b_gate.reshape(1, d), g_post.reshape(1, d))


def _rope_tables(seq):
    half = ROPE_DIM // 2
    inv_freq = ROPE_THETA ** (-jnp.arange(half, dtype=F32) / half)
    ang = jnp.arange(seq, dtype=F32)[:, None] * inv_freq[None, :]
    cos, sin = jnp.cos(ang), jnp.sin(ang)
    ones = jnp.ones((seq, HEAD_DIM - ROPE_DIM), F32)
    zeros = jnp.zeros((seq, HEAD_DIM - half), F32)
    cos_h = jnp.concatenate([cos, cos, ones], axis=1)
    sin_up = jnp.concatenate([-sin, zeros], axis=1)
    sin_dn = jnp.concatenate([jnp.zeros((seq, half), F32), sin, jnp.zeros((seq, HEAD_DIM - ROPE_DIM), F32)], axis=1)
    rep = LANES // HEAD_DIM
    lane_tabs = (jnp.tile(cos_h, (1, rep)), jnp.tile(sin_up, (1, rep)), jnp.tile(sin_dn, (1, rep)))
    return lane_tabs, (cos.T, sin.T)


def _rope_lanes(x, cos, sin_up, sin_dn):
    half = ROPE_DIM // 2
    return (x * cos + pltpu.roll(x, LANES - half, axis=1) * sin_up + pltpu.roll(x, half, axis=1) * sin_dn)


def _kv_kernel(hn_ref, wk_ref, bk_ref, wvt_ref, bvt_ref, cos_ref, sup_ref, sdn_ref, k_ref, vt_ref):
    hn = hn_ref[...]
    k = jnp.dot(hn, wk_ref[...], preferred_element_type=F32) + bk_ref[...]
    cos, sup, sdn = cos_ref[...], sup_ref[...], sdn_ref[...]
    heads_per_tile = LANES // HEAD_DIM
    for c in range(k.shape[-1] // LANES):
        kr = _rope_lanes(k[:, c * LANES:(c + 1) * LANES], cos, sup, sdn).astype(BF16)
        for e in range(heads_per_tile):
            k_ref[c * heads_per_tile + e] = kr[:, e * HEAD_DIM:(e + 1) * HEAD_DIM]
    vt = lax.dot_general(wvt_ref[...], hn, _NT, preferred_element_type=F32) + bvt_ref[...]
    vt_ref[...] = vt.astype(BF16)


def _shared_kv(hn, seq, w_kv, b_kv, lane_tabs, tm=512):
    t, d = hn.shape
    kw = w_kv.shape[1] // 2
    n_kv = kw // HEAD_DIM
    assert kw % LANES == 0
    tm = _row_tile(seq, tm)
    row = pl.BlockSpec((tm, d), lambda i: (i, 0))
    tab = pl.BlockSpec((tm, LANES), lambda i: (i % (seq // tm), 0))
    wk = w_kv[:, :kw].astype(BF16)
    wvt = w_kv[:, kw:].T.astype(BF16)
    return pl.pallas_call(
        _kv_kernel,
        grid=(t // tm,),
        in_specs=[row, _resident((d, kw)), _resident((1, kw)), _resident((kw, d)), _resident((kw, 1)),
                  tab, tab, tab],
        out_specs=[pl.BlockSpec((n_kv, tm, HEAD_DIM), lambda i: (0, i, 0)), pl.BlockSpec((kw, tm), lambda i: (0, i))],
        out_shape=[jax.ShapeDtypeStruct((n_kv, t, HEAD_DIM), BF16), jax.ShapeDtypeStruct((kw, t), BF16)],
        compiler_params=_compiler_params(("parallel",)),
        name="shared_kv",
    )(hn, wk, b_kv[:kw].reshape(1, kw), wvt, b_kv[kw:].reshape(kw, 1), *lane_tabs)


def _q_kernel(hn_ref, wt_ref, bt_ref, cos_ref, sin_ref, q_ref):
    scale = HEAD_DIM ** -0.5 * LOG2_E
    qt = (lax.dot_general(wt_ref[...], hn_ref[...], _NT, preferred_element_type=F32) + bt_ref[...]) * scale
    q_ref[...] = qt.astype(BF16)
    cos, sin = cos_ref[...], sin_ref[...]
    half = ROPE_DIM // 2
    for head in range(qt.shape[0] // HEAD_DIM):
        r0 = head * HEAD_DIM
        t1, t2 = qt[r0:r0 + half], qt[r0 + half:r0 + ROPE_DIM]
        rot = jnp.concatenate([t1 * cos - t2 * sin, t2 * cos + t1 * sin], axis=0)
        q_ref[r0:r0 + ROPE_DIM, :] = rot.astype(BF16)


def _q_proj(hn, seq, w_q, b_q, freq_tabs, tm=512):
    t, d = hn.shape
    tm = _row_tile(seq, tm)
    half = ROPE_DIM // 2
    tab = pl.BlockSpec((half, tm), lambda i: (0, i % (seq // tm)))
    return pl.pallas_call(
        _q_kernel,
        grid=(t // tm,),
        in_specs=[pl.BlockSpec((tm, d), lambda i: (i, 0)), _resident((d, d)), _resident((d, 1)), tab, tab],
        out_specs=pl.BlockSpec((d, tm), lambda i: (0, i)),
        out_shape=jax.ShapeDtypeStruct((d, t), BF16),
        compiler_params=_compiler_params(("parallel",)),
        name="q_proj",
    )(hn, w_q.T.astype(BF16), b_q.reshape(d, 1), *freq_tabs)


def _attend_tile(n_kv, qb, step_in_seq, sink_ref, q_ref, kc_ref, kp_ref, vc_ref, vp_ref, o_ref, interleave=None):
    blk = ATTN_BLOCK
    width = Q_PER_KV * blk
    r = lax.broadcasted_iota(jnp.int32, (blk, width), 0)
    qi = lax.broadcasted_iota(jnp.int32, (blk, width), 1) % blk
    from_prev = r > qi
    has_prev = (step_in_seq > 0) | (r < 0)
    zero = jnp.zeros((blk, width), BF16)
    ones_rows = jnp.ones((SUBLANES, 2 * blk), BF16)
    for j in range(qb):
        lanes = slice(j * blk, (j + 1) * blk)
        for kvh in range(n_kv):
            rows = slice(kvh * HEAD_DIM, (kvh + 1) * HEAD_DIM)
            if j == 0:
                kb = jnp.concatenate([kp_ref[kvh], kc_ref[kvh, :blk]], axis=0)
                vbt = jnp.concatenate([vp_ref[rows, :], vc_ref[rows, :blk]], axis=1)
            else:
                kb = kc_ref[kvh, (j - 1) * blk:(j + 1) * blk]
                vbt = vc_ref[rows, (j - 1) * blk:(j + 1) * blk]
            qcat = jnp.concatenate(
                [q_ref[(kvh * Q_PER_KV + gq) * HEAD_DIM:(kvh * Q_PER_KV + gq + 1) * HEAD_DIM, lanes]
                 for gq in range(Q_PER_KV)], axis=1)
            s2 = jnp.dot(kb, qcat, preferred_element_type=F32)
            s_prev = s2[:blk]
            if j == 0:
                s_prev = jnp.where(has_prev, s_prev, MASK_VALUE)
            s = jnp.where(from_prev, s_prev, s2[blk:])
            sink = sink_ref[kvh]
            m = jnp.maximum(jnp.max(s, axis=0, keepdims=True), sink)
            p = jnp.exp2(s - m).astype(BF16)
            p2 = jnp.concatenate([jnp.where(from_prev, p, zero), jnp.where(from_prev, zero, p)], axis=0)
            ov = jnp.dot(jnp.concatenate([vbt, ones_rows], axis=0), p2, preferred_element_type=F32)
            den = ov[HEAD_DIM:HEAD_DIM + 1] + jnp.exp2(sink - m)
            o = ov[:HEAD_DIM] * (1.0 / den)
            for gq in range(Q_PER_KV):
                r0 = (kvh * Q_PER_KV + gq) * HEAD_DIM
                o_ref[r0:r0 + HEAD_DIM, lanes] = o[:, gq * blk:(gq + 1) * blk].astype(BF16)
            if interleave is not None:
                interleave(j * n_kv + kvh, qb * n_kv)


def _attn_o_kernel(n_kv, qb, steps_per_seq, sink_ref, q_ref, kc_ref, kp_ref, vc_ref, vp_ref, h_ref, w_ref, b_ref,
                   g_ref, out_ref, at_even_ref, at_odd_ref, mix_ref):
    s = pl.program_id(0)
    tile = jnp.minimum(s, pl.num_programs(0) - 2)

    @pl.when(s == 0)
    def _():
        at_odd_ref[...] = jnp.zeros_like(at_odd_ref)

    def stage(done_ref, next_ref):
        d = w_ref.shape[1]
        a_prev = done_ref[...].T

        def project_chunk(i, n):
            if i % 2 == 0:
                return
            width = d // (n // 2)
            cols = slice((i // 2) * width, (i // 2 + 1) * width)
            mix_ref[:, cols] = jnp.dot(a_prev, w_ref[:, cols], preferred_element_type=F32) + b_ref[:, cols]

        _attend_tile(n_kv, qb, tile % steps_per_seq, sink_ref, q_ref, kc_ref, kp_ref, vc_ref, vp_ref, next_ref,
                     interleave=project_chunk)
        out_ref[...] = h_ref[...] + _rms_norm(mix_ref[...], g_ref[...])

    last = pl.num_programs(0) - 1

    @pl.when((s % 2 == 0) & (s < last))
    def _():
        stage(at_odd_ref, at_even_ref)

    @pl.when((s % 2 == 1) & (s < last))
    def _():
        stage(at_even_ref, at_odd_ref)

    def drain(done_ref):
        mix = jnp.dot(done_ref[...].T, w_ref[...], preferred_element_type=F32) + b_ref[...]
        out_ref[...] = h_ref[...] + _rms_norm(mix, g_ref[...])

    @pl.when((s == last) & (s % 2 == 0))
    def _():
        drain(at_odd_ref)

    @pl.when((s == last) & (s % 2 == 1))
    def _():
        drain(at_even_ref)


def _attention_o_proj(qt, k, vt, sinks, h, w_o, b_o, g_post, bsz, seq, qb=4):
    d, t = qt.shape
    n_kv = k.shape[0]
    blk = ATTN_BLOCK
    qb = min(qb, seq // blk)
    assert seq % (qb * blk) == 0
    tq = qb * blk
    steps = seq // tq
    n_tiles = bsz * steps
    sink_x = jnp.repeat(sinks.astype(F32) * LOG2_E, blk).reshape(n_kv, 1, Q_PER_KV * blk)
    cur = lambda s: jnp.minimum(s, n_tiles - 1)
    prev = lambda s: (cur(s) // steps) * steps * qb + jnp.maximum((cur(s) % steps) * qb - 1, 0)
    done = lambda s: jnp.maximum(s - 1, 0)
    return pl.pallas_call(
        functools.partial(_attn_o_kernel, n_kv, qb, steps),
        grid=(n_tiles + 1,),
        in_specs=[_resident((n_kv, 1, Q_PER_KV * blk)),
                  pl.BlockSpec((d, tq), lambda s: (0, cur(s))),
                  pl.BlockSpec((n_kv, tq, HEAD_DIM), lambda s: (0, cur(s), 0)),
                  pl.BlockSpec((n_kv, blk, HEAD_DIM), lambda s: (0, prev(s), 0)),
                  pl.BlockSpec((n_kv * HEAD_DIM, tq), lambda s: (0, cur(s))),
                  pl.BlockSpec((n_kv * HEAD_DIM, blk), lambda s: (0, prev(s))),
                  pl.BlockSpec((tq, d), lambda s: (done(s), 0)),
                  _resident((d, d)), _resident((1, d)), _resident((1, d))],
        out_specs=pl.BlockSpec((tq, d), lambda s: (done(s), 0)),
        out_shape=jax.ShapeDtypeStruct((t, d), F32),
        scratch_shapes=[pltpu.VMEM((d, tq), BF16), pltpu.VMEM((d, tq), BF16), pltpu.VMEM((tq, d), F32)],
        compiler_params=_compiler_params(("arbitrary",)),
        name="swa_sink_attention_o_proj",
    )(sink_x, qt, k, k, vt, vt, h, w_o.astype(BF16), b_o.reshape(1, d), g_post.reshape(1, d))


def kernel(x, norm_g, ffn_w_gate, ffn_w_up, ffn_w_down, ssm_a_re, ssm_a_im, ssm_log_dt, ssm_b_re, ssm_b_im, ssm_c_re, ssm_c_im, ssm_d, glu_w_out, glu_w_gate, glu_b_gate, kv_norm_g, w_kv, b_kv, w_q, b_q, attn_sinks, w_o, b_o):
    bsz, seq, d = x.shape
    depth = norm_g.shape[0]
    n_a = ssm_a_re.shape[0]
    assert seq % ATTN_BLOCK == 0 and seq % SSM_CHUNK == 0 and d % LANES == 0
    h = x.astype(F32).reshape(bsz * seq, d)
    lane_tabs, freq_tabs = _rope_tables(seq)
    ffn_order = [(layer, which) for layer in range(depth) for which in range(2)]
    weights = tuple(w[0, 0].astype(BF16) for w in (ffn_w_gate, ffn_w_up, ffn_w_down))

    def following(layer, which):
        pos = ffn_order.index((layer, which)) + 1
        return (ffn_w_gate, ffn_w_up, ffn_w_down) + ffn_order[pos] if pos < len(ffn_order) else None

    k = vt = None
    for layer in range(depth):
        g = norm_g[layer].astype(F32)
        h, hn, weights = _ffn(h, g[0], g[1], weights, g_next=g[2], next_weights=following(layer, 0))
        if layer < n_a:
            ssm = _s5_params(ssm_a_re[layer], ssm_a_im[layer], ssm_log_dt[layer], ssm_b_re[layer].astype(F32),
                             ssm_b_im[layer].astype(F32), ssm_c_re[layer].astype(F32), ssm_c_im[layer].astype(F32))
            z = _s5_core(hn, bsz, seq, ssm, ssm_d[layer])
            h = _glu(z, h, glu_w_out[layer].astype(BF16), glu_w_gate[layer].astype(BF16), glu_b_gate[layer], g[3])
        else:
            bl = layer - n_a
            qt = _q_proj(hn, seq, w_q[bl], b_q[bl], freq_tabs)
            h = _attention_o_proj(qt, k, vt, attn_sinks[bl], h, w_o[bl], b_o[bl], g[3], bsz, seq)
        g_kv = kv_norm_g.astype(F32) if layer == n_a - 1 else None
        h, hn_kv, weights = _ffn(h, g[4], g[5], weights, g_next=g_kv, next_weights=following(layer, 1))
        if layer == n_a - 1:
            k, vt = _shared_kv(hn_kv, seq, w_kv, b_kv, lane_tabs)
    return h.reshape(bsz, seq, d).astype(x.dtype)
```
